```python
import jax, jax.numpy as jnp
from jax import lax
import numpy as np

D_MODEL = 4096
BATCH = 1
SEQ = 16384
DEPTH = 4

N_MIXERS = 2
HEAD_DIM = 128
N_HEADS = D_MODEL // HEAD_DIM
HGRN_CHUNK = 64
GMLP_CHUNK = 128
GMLP_HALF = D_MODEL
GMLP_GROUPS = GMLP_HALF // HEAD_DIM
D_FF = 4 * D_MODEL
N_HGRN_LAYERS = (DEPTH + N_MIXERS - 1) // N_MIXERS
N_GMLP_LAYERS = DEPTH // N_MIXERS
EPS = 1e-5

kernel_name = "hgrn2_gmlp_interleaved_trunk"


def rms_norm(x, w):
    xf = x.astype(jnp.float32)
    y = xf * lax.rsqrt(jnp.mean(xf * xf, axis=-1, keepdims=True) + EPS)
    return (y * w.astype(jnp.float32)).astype(x.dtype)


def _hgrn2_chunk(state, inp):
    q, k, v, logf = inp
    b = jnp.cumsum(logf, axis=2)
    c = q.shape[2]
    causal = jnp.tril(jnp.ones((c, c), dtype=bool))[None, None, :, :, None]
    diff = b[:, :, :, None, :] - b[:, :, None, :, :]
    decay = jnp.exp(jnp.where(causal, diff, -jnp.inf))
    scores = jnp.einsum('bhtk,bhtsk,bhsk->bhts', q, decay, k)
    o = jnp.einsum('bhts,bhsv->bhtv', scores, v) \
        + jnp.einsum('bhtk,bhkv->bhtv', q * jnp.exp(b), state)
    b_last = b[:, :, -1:, :]
    new_state = jnp.exp(b_last[:, :, 0, :])[..., None] * state \
        + jnp.einsum('bhsk,bhsv->bhkv', k * jnp.exp(b_last - b), v)
    return new_state, o


def hgrn2_mixer(xn, w_in, lb, gnorm_w, w_out):
    B, S, _ = xn.shape
    proj = xn @ w_in
    q, f, i, gate = jnp.split(proj, 4, axis=-1)
    q = jax.nn.silu(q.astype(jnp.float32))
    forget = lb + (1.0 - lb) * jax.nn.sigmoid(f.astype(jnp.float32))
    k = 1.0 - forget
    logf = jnp.log(forget)
    nc = S // HGRN_CHUNK

    def to_chunks(t):
        return t.reshape(B, nc, HGRN_CHUNK, N_HEADS, HEAD_DIM).transpose(1, 0, 3, 2, 4)

    state0 = jnp.zeros((B, N_HEADS, HEAD_DIM, HEAD_DIM), jnp.float32)
    _, o = lax.scan(_hgrn2_chunk, state0,
                    (to_chunks(q), to_chunks(k), to_chunks(i.astype(jnp.float32)), to_chunks(logf)))
    o = o.transpose(1, 0, 3, 2, 4).reshape(B, S, N_HEADS, HEAD_DIM)
    o = o * lax.rsqrt(jnp.mean(o * o, axis=-1, keepdims=True) + EPS) * gnorm_w.astype(jnp.float32)
    o = o * jax.nn.silu(gate.astype(jnp.float32).reshape(B, S, N_HEADS, HEAD_DIM))
    return o.reshape(B, S, D_MODEL).astype(xn.dtype) @ w_out


def gmlp_mixer(xn, w_in, ln_g, ln_b, w_sp, b_sp, w_out):
    B, S, _ = xn.shape
    z = jax.nn.gelu(xn @ w_in)
    u, v = jnp.split(z, 2, axis=-1)
    vf = v.astype(jnp.float32)
    mu = jnp.mean(vf, axis=-1, keepdims=True)
    var = jnp.mean(jnp.square(vf - mu), axis=-1, keepdims=True)
    vf = (vf - mu) * lax.rsqrt(var + EPS) * ln_g.astype(jnp.float32) + ln_b.astype(jnp.float32)
    nc = S // GMLP_CHUNK
    vf = vf.reshape(B, nc, GMLP_CHUNK, GMLP_GROUPS, HEAD_DIM)
    w_causal = jnp.tril(w_sp.astype(jnp.float32))
    mixed = jnp.einsum('gts,bcsgd->bctgd', w_causal, vf) \
        + b_sp.astype(jnp.float32).T[None, None, :, :, None]
    out = u.astype(jnp.float32) * mixed.reshape(B, S, GMLP_HALF)
    return out.astype(xn.dtype) @ w_out


def sq_relu_mlp(xn, w1, w2):
    return jnp.square(jax.nn.relu(xn @ w1)) @ w2


def setup_inputs(seed: int = 0) -> dict:
    key = jax.random.key(seed)
    ks = jax.random.split(key, 16)
    f32 = jnp.float32
    nrm = lambda k, shape, s: jax.random.normal(k, shape, f32) * s
    return {
        "x": nrm(ks[0], (BATCH, SEQ, D_MODEL), 1.0),
        "norm_mix": 1.0 + nrm(ks[1], (DEPTH, D_MODEL), 0.01),
        "norm_ffn": 1.0 + nrm(ks[2], (DEPTH, D_MODEL), 0.01),
        "hgrn_w_in": nrm(ks[3], (N_HGRN_LAYERS, D_MODEL, 4 * D_MODEL), D_MODEL ** -0.5),
        "hgrn_lower_bounds": nrm(ks[4], (DEPTH, D_MODEL), 0.1),
        "hgrn_gnorm": 1.0 + nrm(ks[5], (N_HGRN_LAYERS, HEAD_DIM), 0.01),
        "hgrn_w_out": nrm(ks[6], (N_HGRN_LAYERS, D_MODEL, D_MODEL), D_MODEL ** -0.5),
        "gmlp_w_in": nrm(ks[7], (N_GMLP_LAYERS, D_MODEL, 2 * GMLP_HALF), D_MODEL ** -0.5),
        "gmlp_ln_gain": 1.0 + nrm(ks[8], (N_GMLP_LAYERS, GMLP_HALF), 0.01),
        "gmlp_ln_bias": nrm(ks[9], (N_GMLP_LAYERS, GMLP_HALF), 0.01),
        "gmlp_w_spatial": nrm(ks[10], (N_GMLP_LAYERS, GMLP_GROUPS, GMLP_CHUNK, GMLP_CHUNK), GMLP_CHUNK ** -0.5),
        "gmlp_b_spatial": 1.0 + nrm(ks[11], (N_GMLP_LAYERS, GMLP_GROUPS, GMLP_CHUNK), 0.1),
        "gmlp_w_out": nrm(ks[12], (N_GMLP_LAYERS, GMLP_HALF, D_MODEL), GMLP_HALF ** -0.5),
        "ffn_w1": nrm(ks[13], (DEPTH, D_MODEL, D_FF), D_MODEL ** -0.5),
        "ffn_w2": nrm(ks[14], (DEPTH, D_FF, D_MODEL), D_FF ** -0.5),
        "norm_final": 1.0 + nrm(ks[15], (D_MODEL,), 0.01),
    }


def reference(x, norm_mix, norm_ffn, hgrn_w_in, hgrn_lower_bounds, hgrn_gnorm, hgrn_w_out,
              gmlp_w_in, gmlp_ln_gain, gmlp_ln_bias, gmlp_w_spatial, gmlp_b_spatial, gmlp_w_out,
              ffn_w1, ffn_w2, norm_final):
    p = jax.nn.softmax(hgrn_lower_bounds.astype(jnp.float32), axis=0)
    lower_bounds = jnp.cumsum(p, axis=0) - p[0]
    h = x
    for layer in range(DEPTH):
        j = layer // N_MIXERS
        xn = rms_norm(h, norm_mix[layer])
        if layer % N_MIXERS == 0:
            h = h + hgrn2_mixer(xn, hgrn_w_in[j], lower_bounds[layer], hgrn_gnorm[j], hgrn_w_out[j])
        else:
            h = h + gmlp_mixer(xn, gmlp_w_in[j], gmlp_ln_gain[j], gmlp_ln_bias[j],
                               gmlp_w_spatial[j], gmlp_b_spatial[j], gmlp_w_out[j])
        h = h + sq_relu_mlp(rms_norm(h, norm_ffn[layer]), ffn_w1[layer], ffn_w2[layer])
    return rms_norm(h, norm_final)
```

```python
import functools
import math

import numpy as np
import jax
import jax.numpy as jnp
from jax import lax
from jax.experimental import pallas as pl
from jax.experimental.pallas import tpu as pltpu

HEAD_DIM = 128
GMLP_CHUNK = 128
N_MIXERS = 2
EPS = 1e-5
HGRN_CHUNK = 128

V7X_VMEM_LIMIT_BYTES = 56 * 1024 * 1024

F32 = jnp.float32
BF16 = jnp.bfloat16


def _params(sem):
    return pltpu.CompilerParams(dimension_semantics=sem, vmem_limit_bytes=V7X_VMEM_LIMIT_BYTES)


def _sigmoid(x):
    return 1.0 / (1.0 + jnp.exp(-x))


def _lower_bounds_kernel(lb_ref, o_ref):
    x = lb_ref[...]
    m = jnp.max(x, axis=0, keepdims=True)
    e = jnp.exp(x - m)
    p = e / jnp.sum(e, axis=0, keepdims=True)
    depth = x.shape[0]
    run = p[0:1, :]
    rows = [run - p[0:1, :]]
    for l in range(1, depth):
        run = run + p[l:l + 1, :]
        rows.append(run - p[0:1, :])
    o_ref[...] = jnp.concatenate(rows, axis=0)


def _lower_bounds(lb):
    return pl.pallas_call(
        _lower_bounds_kernel,
        out_shape=jax.ShapeDtypeStruct(lb.shape, F32),
        name="lower_bounds",
    )(lb.astype(F32))


def _rmsnorm_kernel(x_ref, w_ref, o_ref):
    x = x_ref[...]
    ms = jnp.mean(x * x, axis=-1, keepdims=True)
    o_ref[...] = (x * lax.rsqrt(ms + EPS) * w_ref[...]).astype(o_ref.dtype)


def _rmsnorm(x, w, out_dtype):
    s, d = x.shape
    bm = min(256, s)
    return pl.pallas_call(
        _rmsnorm_kernel,
        grid=(s // bm,),
        in_specs=[pl.BlockSpec((bm, d), lambda i: (i, 0)),
                  pl.BlockSpec((1, d), lambda i: (0, 0))],
        out_specs=pl.BlockSpec((bm, d), lambda i: (i, 0)),
        out_shape=jax.ShapeDtypeStruct((s, d), out_dtype),
        compiler_params=_params(("parallel",)),
        name="rmsnorm",
    )(x, w.reshape(1, d).astype(F32))


def _ep_silu(acc):
    return (acc * _sigmoid(acc),)


def _ep_identity(acc):
    return (acc,)


def _ep_forget(acc, lb):
    forget = lb + (1.0 - lb) * _sigmoid(acc)
    return (jnp.log(forget), 1.0 - forget)


def _ep_gelu(acc):
    return (jax.nn.gelu(acc),)


def _ep_relu2(acc):
    r = jnp.maximum(acc, 0.0)
    return (r * r,)


def _ep_residual(acc, res):
    return (res + acc,)


def _mm_kernel(a_ref, b_ref, *rest, epilogue, n_extra):
    extra = rest[:n_extra]
    outs = rest[n_extra:]
    acc = jnp.dot(a_ref[...], b_ref[...], preferred_element_type=F32)
    res = epilogue(acc, *[e[...] for e in extra])
    for o_ref, r in zip(outs, res):
        o_ref[...] = r.astype(o_ref.dtype)


def _mm_kgrid_kernel(a_ref, b_ref, *rest, epilogue, n_extra, nk):
    extra = rest[:n_extra]
    outs = rest[n_extra:-1]
    acc_ref = rest[-1]
    k = pl.program_id(2)

    @pl.when(k == 0)
    def _():
        acc_ref[...] = jnp.zeros_like(acc_ref)

    acc_ref[...] += jnp.dot(a_ref[...], b_ref[...], preferred_element_type=F32)

    @pl.when(k == nk - 1)
    def _():
        res = epilogue(acc_ref[...], *[e[...] for e in extra])
        for o_ref, r in zip(outs, res):
            o_ref[...] = r.astype(o_ref.dtype)


def _matmul(a, b, *, n_off=0, n=None, epilogue, extra=(), out_dtypes, bm=1024, bn=1024, bk=None, name):
    m, kdim = a.shape
    n = b.shape[1] if n is None else n
    bm, bn = min(bm, m), min(bn, n)
    bk = kdim if bk is None else min(bk, kdim)
    nk = kdim // bk
    off = n_off // bn
    extra_specs = []
    extra_arrays = []
    for kind, arr in extra:
        if kind == "row":
            extra_specs.append(pl.BlockSpec((1, bn), lambda i, j, *_: (0, j)))
        else:
            extra_specs.append(pl.BlockSpec((bm, bn), lambda i, j, *_: (i, j)))
        extra_arrays.append(arr)
    out_shape = tuple(jax.ShapeDtypeStruct((m, n), dt) for dt in out_dtypes)
    out_specs = tuple(pl.BlockSpec((bm, bn), lambda i, j, *_: (i, j)) for _ in out_dtypes)
    if nk == 1:
        body = functools.partial(_mm_kernel, epilogue=epilogue, n_extra=len(extra))
        grid = (m // bm, n // bn)
        in_specs = [pl.BlockSpec((bm, kdim), lambda i, j: (i, 0)),
                    pl.BlockSpec((kdim, bn), lambda i, j: (0, j + off))] + extra_specs
        scratch = []
        sem = ("parallel", "parallel")
    else:
        body = functools.partial(_mm_kgrid_kernel, epilogue=epilogue, n_extra=len(extra), nk=nk)
        grid = (m // bm, n // bn, nk)
        in_specs = [pl.BlockSpec((bm, bk), lambda i, j, k: (i, k)),
                    pl.BlockSpec((bk, bn), lambda i, j, k: (k, j + off))] + extra_specs
        scratch = [pltpu.VMEM((bm, bn), F32)]
        sem = ("parallel", "parallel", "arbitrary")
    outs = pl.pallas_call(
        body,
        grid=grid,
        in_specs=in_specs,
        out_specs=out_specs,
        out_shape=out_shape,
        scratch_shapes=scratch,
        compiler_params=_params(sem),
        name=name,
    )(a, b, *extra_arrays)
    return outs


def _hgrn_constants(c):
    n_levels = int(math.log2(c))
    t = np.arange(c)[:, None]
    j = np.arange(c)[None, :]
    blocks = [(j <= t), (j > t)]
    level = np.full((c, c), -1, np.int32)
    for l in range(n_levels):
        m = c >> (l + 1)
        r = (t // (2 * m)) * (2 * m) + m - 1
        second = (t % (2 * m)) >= m
        blocks.append(np.where(second, (j > r) & (j <= t), (j > t) & (j <= r)))
        same = (t // (2 * m)) == (j // (2 * m))
        level[same & second & ((j % (2 * m)) < m)] = l
    level[np.arange(c), np.arange(c)] = n_levels
    cst = np.concatenate(blocks, axis=0).astype(np.float32)
    return np.concatenate([cst, cst], axis=1), level, n_levels


def _dot_nt(a, b):
    return lax.dot_general(a, b, (((1,), (1,)), ((), ())), preferred_element_type=F32)


def _dot_tn(a, b):
    return lax.dot_general(a, b, (((0,), (0,)), ((), ())), preferred_element_type=F32)


def _hgrn_kernel(q_ref, k_ref, v_ref, g_ref, lf_ref, cm_ref, lv_ref, gn_ref, o_ref, st_ref, *, c, n_levels):
    @pl.when(pl.program_id(1) == 0)
    def _():
        st_ref[...] = jnp.zeros_like(st_ref)

    t_blk, w = q_ref.shape
    n_heads = w // HEAD_DIM
    lv = lv_ref[...]
    gn = gn_ref[...]

    def chunk(ci, carry):
        r0 = pl.multiple_of(ci * c, c)
        rows = pl.ds(r0, c)
        lf = lf_ref[rows, :]
        hi = lf.astype(BF16)
        lo = (lf - hi.astype(F32)).astype(BF16)
        ex = jnp.dot(cm_ref[...], jnp.concatenate([hi, lo], axis=0), preferred_element_type=F32)
        dec = jnp.exp(ex)
        for h in range(n_heads):
            sl = slice(h * HEAD_DIM, (h + 1) * HEAD_DIM)
            q = q_ref[rows, sl].astype(F32)
            k = k_ref[rows, sl].astype(F32)
            v = v_ref[rows, sl]
            st = st_ref[h]
            o = _dot_nt((q * dec[0:c, sl]).astype(BF16), st.astype(BF16))
            s = jnp.where(lv == n_levels, _dot_nt(q.astype(BF16), k.astype(BF16)), 0.0)
            for l in range(n_levels):
                d = dec[(2 + l) * c:(3 + l) * c, sl]
                s = jnp.where(lv == l, _dot_nt((q * d).astype(BF16), (k * d).astype(BF16)), s)
            o = o + jnp.dot(s.astype(BF16), v, preferred_element_type=F32)
            ku = (k * dec[c:2 * c, sl]).astype(BF16)
            st_ref[h] = st * dec[c - 1:c, sl] + _dot_tn(v, ku)
            ms = jnp.mean(o * o, axis=-1, keepdims=True)
            o = o * lax.rsqrt(ms + EPS) * gn * g_ref[rows, sl].astype(F32)
            o_ref[rows, sl] = o.astype(o_ref.dtype)
        return carry

    lax.fori_loop(0, t_blk // c, chunk, 0)


def _hgrn_recurrence(q, k, v, g, lf, gnorm, *, c=HGRN_CHUNK, t_blk=512, w=512):
    s, d = q.shape
    c = min(c, s)
    t_blk = min(t_blk, s)
    w = min(w, d)
    cm, lv, n_levels = _hgrn_constants(c)
    blk = pl.BlockSpec((t_blk, w), lambda hi, ti: (ti, hi))
    return pl.pallas_call(
        functools.partial(_hgrn_kernel, c=c, n_levels=n_levels),
        grid=(d // w, s // t_blk),
        in_specs=[blk, blk, blk, blk, blk,
                  pl.BlockSpec(cm.shape, lambda hi, ti: (0, 0)),
                  pl.BlockSpec(lv.shape, lambda hi, ti: (0, 0)),
                  pl.BlockSpec((1, HEAD_DIM), lambda hi, ti: (0, 0))],
        out_specs=blk,
        out_shape=jax.ShapeDtypeStruct((s, d), BF16),
        scratch_shapes=[pltpu.VMEM((w // HEAD_DIM, HEAD_DIM, HEAD_DIM), F32)],
        compiler_params=_params(("parallel", "arbitrary")),
        name="hgrn_recurrence",
    )(q, k, v, g, lf, jnp.asarray(cm, BF16), jnp.asarray(lv), gnorm.reshape(1, HEAD_DIM).astype(F32))


def _gmlp_kernel(z_ref, lng_ref, lnb_ref, wsp_ref, bsp_ref, o_ref, wc_ref):
    @pl.when(pl.program_id(0) == 0)
    def _():
        n = wsp_ref.shape[1]
        tri = lax.broadcasted_iota(jnp.int32, (n, n), 0) >= lax.broadcasted_iota(jnp.int32, (n, n), 1)
        for g in range(wsp_ref.shape[0]):
            wc_ref[g] = jnp.where(tri, wsp_ref[g], 0.0).astype(BF16)

    t_blk, d2 = z_ref.shape
    d = d2 // 2
    n_groups = d // HEAD_DIM
    for ci in range(t_blk // GMLP_CHUNK):
        rows = slice(ci * GMLP_CHUNK, (ci + 1) * GMLP_CHUNK)
        v = z_ref[rows, d:].astype(F32)
        mu = jnp.mean(v, axis=-1, keepdims=True)
        vc = v - mu
        var = jnp.mean(vc * vc, axis=-1, keepdims=True)
        rstd = lax.rsqrt(var + EPS)
        for g in range(n_groups):
            sl = slice(g * HEAD_DIM, (g + 1) * HEAD_DIM)
            vn = vc[:, sl] * rstd * lng_ref[:, sl] + lnb_ref[:, sl]
            mixed = jnp.dot(wc_ref[g], vn.astype(BF16), preferred_element_type=F32) + bsp_ref[g]
            o_ref[rows, sl] = (z_ref[rows, sl].astype(F32) * mixed).astype(o_ref.dtype)


def _gmlp_mix(z, ln_g, ln_b, w_sp, b_sp, *, t_blk=256):
    s, d2 = z.shape
    d = d2 // 2
    g, n, _ = w_sp.shape
    t_blk = min(t_blk, s)
    bsp = jnp.broadcast_to(b_sp.astype(F32)[:, :, None], (g, n, HEAD_DIM))
    return pl.pallas_call(
        _gmlp_kernel,
        grid=(s // t_blk,),
        in_specs=[pl.BlockSpec((t_blk, d2), lambda i: (i, 0)),
                  pl.BlockSpec((1, d), lambda i: (0, 0)),
                  pl.BlockSpec((1, d), lambda i: (0, 0)),
                  pl.BlockSpec((g, n, n), lambda i: (0, 0, 0)),
                  pl.BlockSpec((g, n, HEAD_DIM), lambda i: (0, 0, 0))],
        out_specs=pl.BlockSpec((t_blk, d), lambda i: (i, 0)),
        out_shape=jax.ShapeDtypeStruct((s, d), BF16),
        scratch_shapes=[pltpu.VMEM((g, n, n), BF16)],
        compiler_params=_params(("arbitrary",)),
        name="gmlp_mix",
    )(z, ln_g.reshape(1, d).astype(F32), ln_b.reshape(1, d).astype(F32), w_sp.astype(F32), bsp)


def kernel(x, norm_mix, norm_ffn, hgrn_w_in, hgrn_lower_bounds, hgrn_gnorm, hgrn_w_out, gmlp_w_in,
           gmlp_ln_gain, gmlp_ln_bias, gmlp_w_spatial, gmlp_b_spatial, gmlp_w_out, ffn_w1, ffn_w2, norm_final):
    bsz, s, d = x.shape
    assert bsz == 1, "the recurrence kernel carries one sequence"
    depth = norm_mix.shape[0]
    h = x.reshape(s, d).astype(F32)
    lower_bounds = _lower_bounds(hgrn_lower_bounds)

    for layer in range(depth):
        j = layer // N_MIXERS
        xn = _rmsnorm(h, norm_mix[layer], BF16)
        if layer % N_MIXERS == 0:
            w_in = hgrn_w_in[j].astype(BF16)
            lb = lower_bounds[layer].reshape(1, d)
            (q,) = _matmul(xn, w_in, n_off=0, n=d, epilogue=_ep_silu, out_dtypes=(BF16,), name="hgrn_q")
            lf, k = _matmul(xn, w_in, n_off=d, n=d, epilogue=_ep_forget, extra=(("row", lb),),
                            out_dtypes=(F32, BF16), name="hgrn_f")
            (v,) = _matmul(xn, w_in, n_off=2 * d, n=d, epilogue=_ep_identity, out_dtypes=(BF16,), name="hgrn_i")
            (g,) = _matmul(xn, w_in, n_off=3 * d, n=d, epilogue=_ep_silu, out_dtypes=(BF16,), name="hgrn_g")
            mixed = _hgrn_recurrence(q, k, v, g, lf, hgrn_gnorm[j])
            w_out = hgrn_w_out[j].astype(BF16)
        else:
            (z,) = _matmul(xn, gmlp_w_in[j].astype(BF16), epilogue=_ep_gelu, out_dtypes=(BF16,), name="gmlp_in")
            mixed = _gmlp_mix(z, gmlp_ln_gain[j], gmlp_ln_bias[j], gmlp_w_spatial[j], gmlp_b_spatial[j])
            w_out = gmlp_w_out[j].astype(BF16)
        (h,) = _matmul(mixed, w_out, epilogue=_ep_residual, extra=(("full", h),), out_dtypes=(F32,),
                       bn=512, name="mix_out")
        xn = _rmsnorm(h, norm_ffn[layer], BF16)
        (hid,) = _matmul(xn, ffn_w1[layer].astype(BF16), epilogue=_ep_relu2, out_dtypes=(BF16,), name="ffn_up")
        (h,) = _matmul(hid, ffn_w2[layer].astype(BF16), epilogue=_ep_residual, extra=(("full", h),),
                       out_dtypes=(F32,), bk=2048, name="ffn_down")
    out = _rmsnorm(h, norm_final, F32)
    return out.reshape(bsz, s, d)
```

```python
import functools
import math

import numpy as np
import jax
import jax.numpy as jnp
from jax import lax
from jax.experimental import pallas as pl
from jax.experimental.pallas import tpu as pltpu

HEAD_DIM = 128
GMLP_CHUNK = 128
N_MIXERS = 2
EPS = 1e-5
HGRN_CHUNK = 128

V7X_VMEM_LIMIT_BYTES = 56 * 1024 * 1024

F32 = jnp.float32
BF16 = jnp.bfloat16


def _params(sem):
    return pltpu.CompilerParams(dimension_semantics=sem, vmem_limit_bytes=V7X_VMEM_LIMIT_BYTES)


def _sigmoid(x):
    return 1.0 / (1.0 + jnp.exp(-x))


def _lower_bounds_kernel(lb_ref, o_ref):
    x = lb_ref[...]
    m = jnp.max(x, axis=0, keepdims=True)
    e = jnp.exp(x - m)
    p = e / jnp.sum(e, axis=0, keepdims=True)
    depth = x.shape[0]
    run = p[0:1, :]
    rows = [run - p[0:1, :]]
    for l in range(1, depth):
        run = run + p[l:l + 1, :]
        rows.append(run - p[0:1, :])
    o_ref[...] = jnp.concatenate(rows, axis=0)


def _lower_bounds(lb):
    return pl.pallas_call(
        _lower_bounds_kernel,
        out_shape=jax.ShapeDtypeStruct(lb.shape, F32),
        name="lower_bounds",
    )(lb.astype(F32))


def _row_stats_kernel(x_ref, xb_ref, ss_ref):
    x = x_ref[...]
    xb_ref[...] = x.astype(xb_ref.dtype)
    ss_ref[...] = jnp.sum(x * x, axis=-1, keepdims=True)


def _row_stats(x):
    s, d = x.shape
    bm = min(256, s)
    return pl.pallas_call(
        _row_stats_kernel,
        grid=(s // bm,),
        in_specs=[pl.BlockSpec((bm, d), lambda i: (i, 0))],
        out_specs=(pl.BlockSpec((bm, d), lambda i: (i, 0)), pl.BlockSpec((bm, 1), lambda i: (i, 0))),
        out_shape=(jax.ShapeDtypeStruct((s, d), BF16), jax.ShapeDtypeStruct((s, 1), F32)),
        compiler_params=_params(("parallel",)),
        name="row_stats",
    )(x)


def _rmsnorm_kernel(x_ref, w_ref, o_ref):
    x = x_ref[...]
    ms = jnp.mean(x * x, axis=-1, keepdims=True)
    o_ref[...] = (x * lax.rsqrt(ms + EPS) * w_ref[...]).astype(o_ref.dtype)


def _rmsnorm(x, w, out_dtype):
    s, d = x.shape
    bm = min(256, s)
    return pl.pallas_call(
        _rmsnorm_kernel,
        grid=(s // bm,),
        in_specs=[pl.BlockSpec((bm, d), lambda i: (i, 0)),
                  pl.BlockSpec((1, d), lambda i: (0, 0))],
        out_specs=pl.BlockSpec((bm, d), lambda i: (i, 0)),
        out_shape=jax.ShapeDtypeStruct((s, d), out_dtype),
        compiler_params=_params(("parallel",)),
        name="rmsnorm",
    )(x, w.reshape(1, d).astype(F32))


def _cast_kernel(x_ref, o_ref):
    o_ref[...] = x_ref[...].astype(o_ref.dtype)


def _cast_bf16(w):
    nl, r, c = w.shape
    br = min(r, max(8, (2 * 1024 * 1024) // c))
    return pl.pallas_call(
        _cast_kernel,
        grid=(nl, r // br),
        in_specs=[pl.BlockSpec((None, br, c), lambda l, i: (l, i, 0))],
        out_specs=pl.BlockSpec((None, br, c), lambda l, i: (l, i, 0)),
        out_shape=jax.ShapeDtypeStruct(w.shape, BF16),
        compiler_params=_params(("parallel", "parallel")),
        name="cast_bf16",
    )(w)


def _ep_silu(acc):
    return (acc * _sigmoid(acc),)


def _ep_identity(acc):
    return (acc,)


def _ep_forget(acc, lb):
    forget = lb + (1.0 - lb) * _sigmoid(acc)
    return (jnp.log(forget), 1.0 - forget)


def _ep_gelu(acc):
    return (jax.nn.gelu(acc),)


def _ep_relu2(acc):
    r = jnp.maximum(acc, 0.0)
    return (r * r,)


def _proj_kernel(a_ref, w_ref, nw_ref, ss_ref, *rest, epilogue, n_extra, inv_k):
    extra = rest[:n_extra]
    outs = rest[n_extra:-1]
    wb_ref = rest[-1]

    @pl.when(pl.program_id(1) == 0)
    def _():
        wb_ref[...] = (w_ref[...] * nw_ref[...]).astype(BF16)

    acc = jnp.dot(a_ref[...], wb_ref[...], preferred_element_type=F32)
    rstd = lax.rsqrt(ss_ref[...] * inv_k + EPS)
    res = epilogue(acc * rstd, *[e[...] for e in extra])
    for o_ref, r in zip(outs, res):
        o_ref[...] = r.astype(o_ref.dtype)


def _proj(hb, ss, w, layer, nw, *, n_off=0, n=None, epilogue, extra_rows=(), out_dtypes, bm=1024, bn=512, name):
    m, kdim = hb.shape
    n = w.shape[2] if n is None else n
    bm, bn = min(bm, m), min(bn, n)
    off = n_off // bn
    in_specs = [pl.BlockSpec((bm, kdim), lambda j, i: (i, 0)),
                pl.BlockSpec((None, kdim, bn), lambda j, i: (layer, 0, j + off)),
                pl.BlockSpec((kdim, 1), lambda j, i: (0, 0)),
                pl.BlockSpec((bm, 1), lambda j, i: (i, 0))]
    in_specs += [pl.BlockSpec((1, bn), lambda j, i: (0, j)) for _ in extra_rows]
    return pl.pallas_call(
        functools.partial(_proj_kernel, epilogue=epilogue, n_extra=len(extra_rows), inv_k=1.0 / kdim),
        grid=(n // bn, m // bm),
        in_specs=in_specs,
        out_specs=tuple(pl.BlockSpec((bm, bn), lambda j, i: (i, j)) for _ in out_dtypes),
        out_shape=tuple(jax.ShapeDtypeStruct((m, n), dt) for dt in out_dtypes),
        scratch_shapes=[pltpu.VMEM((kdim, bn), BF16)],
        compiler_params=_params(("parallel", "arbitrary")),
        name=name,
    )(hb, w, nw.reshape(kdim, 1).astype(F32), ss, *extra_rows)


def _emit_residual(acc, res_ref, h_ref, hb_ref, ss_ref, first_col_block):
    h = res_ref[...] + acc
    h_ref[...] = h
    hb_ref[...] = h.astype(hb_ref.dtype)
    part = jnp.sum(h * h, axis=-1, keepdims=True)

    @pl.when(first_col_block)
    def _():
        ss_ref[...] = part

    @pl.when(jnp.logical_not(first_col_block))
    def _():
        ss_ref[...] += part


def _out_kernel(a_ref, w_ref, res_ref, h_ref, hb_ref, ss_ref):
    acc = jnp.dot(a_ref[...], w_ref[...], preferred_element_type=F32)
    _emit_residual(acc, res_ref, h_ref, hb_ref, ss_ref, pl.program_id(1) == 0)


def _out_kgrid_kernel(a_ref, w_ref, res_ref, h_ref, hb_ref, ss_ref, acc_ref, *, nk):
    k = pl.program_id(2)

    @pl.when(k == 0)
    def _():
        acc_ref[...] = jnp.zeros_like(acc_ref)

    acc_ref[...] += jnp.dot(a_ref[...], w_ref[...], preferred_element_type=F32)

    @pl.when(k == nk - 1)
    def _():
        _emit_residual(acc_ref[...], res_ref, h_ref, hb_ref, ss_ref, pl.program_id(1) == 0)


def _out_proj(a, wb, layer, res, *, bm=1024, bn=1024, bk=None, name):
    m, kdim = a.shape
    n = wb.shape[2]
    bm, bn = min(bm, m), min(bn, n)
    bk = kdim if bk is None else min(bk, kdim)
    nk = kdim // bk
    out_shape = (jax.ShapeDtypeStruct((m, n), F32), jax.ShapeDtypeStruct((m, n), BF16),
                 jax.ShapeDtypeStruct((m, 1), F32))
    blk = lambda i, j, *_: (i, j)
    out_specs = (pl.BlockSpec((bm, bn), blk), pl.BlockSpec((bm, bn), blk),
                 pl.BlockSpec((bm, 1), lambda i, j, *_: (i, 0)))
    if nk == 1:
        body = _out_kernel
        grid = (m // bm, n // bn)
        in_specs = [pl.BlockSpec((bm, kdim), lambda i, j: (i, 0)),
                    pl.BlockSpec((None, kdim, bn), lambda i, j: (layer, 0, j)),
                    pl.BlockSpec((bm, bn), blk)]
        scratch = []
        sem = ("parallel", "arbitrary")
    else:
        body = functools.partial(_out_kgrid_kernel, nk=nk)
        grid = (m // bm, n // bn, nk)
        in_specs = [pl.BlockSpec((bm, bk), lambda i, j, k: (i, k)),
                    pl.BlockSpec((None, bk, bn), lambda i, j, k: (layer, k, j)),
                    pl.BlockSpec((bm, bn), blk)]
        scratch = [pltpu.VMEM((bm, bn), F32)]
        sem = ("parallel", "arbitrary", "arbitrary")
    return pl.pallas_call(
        body, grid=grid, in_specs=in_specs, out_specs=out_specs, out_shape=out_shape,
        scratch_shapes=scratch, compiler_params=_params(sem), name=name,
    )(a, wb, res)


def _hgrn_constants(c):
    n_levels = int(math.log2(c))
    t = np.arange(c)[:, None]
    j = np.arange(c)[None, :]
    blocks = [(j <= t), (j > t)]
    level = np.full((c, c), -1, np.int32)
    for l in range(n_levels):
        m = c >> (l + 1)
        r = (t // (2 * m)) * (2 * m) + m - 1
        second = (t % (2 * m)) >= m
        blocks.append(np.where(second, (j > r) & (j <= t), (j > t) & (j <= r)))
        same = (t // (2 * m)) == (j // (2 * m))
        level[same & second & ((j % (2 * m)) < m)] = l
    level[np.arange(c), np.arange(c)] = n_levels
    cst = np.concatenate(blocks, axis=0).astype(np.float32)
    return np.concatenate([cst, cst], axis=1), level, n_levels


def _dot_nt(a, b):
    return lax.dot_general(a, b, (((1,), (1,)), ((), ())), preferred_element_type=F32)


def _dot_tn(a, b):
    return lax.dot_general(a, b, (((0,), (0,)), ((), ())), preferred_element_type=F32)


def _hgrn_kernel(q_ref, k_ref, v_ref, g_ref, lf_ref, cm_ref, lv_ref, gn_ref, o_ref, st_ref, *, c, n_levels):
    @pl.when(pl.program_id(1) == 0)
    def _():
        st_ref[...] = jnp.zeros_like(st_ref)

    t_blk, w = q_ref.shape
    n_heads = w // HEAD_DIM
    lv = lv_ref[...]
    gn = gn_ref[...]

    def chunk(ci, carry):
        r0 = pl.multiple_of(ci * c, c)
        rows = pl.ds(r0, c)
        lf = lf_ref[rows, :]
        hi = lf.astype(BF16)
        lo = (lf - hi.astype(F32)).astype(BF16)
        ex = jnp.dot(cm_ref[...], jnp.concatenate([hi, lo], axis=0), preferred_element_type=F32)
        dec = jnp.exp(ex)
        for h in range(n_heads):
            sl = slice(h * HEAD_DIM, (h + 1) * HEAD_DIM)
            q = q_ref[rows, sl].astype(F32)
            k = k_ref[rows, sl].astype(F32)
            v = v_ref[rows, sl]
            st = st_ref[h]
            o = _dot_nt((q * dec[0:c, sl]).astype(BF16), st.astype(BF16))
            s = jnp.where(lv == n_levels, _dot_nt(q.astype(BF16), k.astype(BF16)), 0.0)
            for l in range(n_levels):
                d = dec[(2 + l) * c:(3 + l) * c, sl]
                s = jnp.where(lv == l, _dot_nt((q * d).astype(BF16), (k * d).astype(BF16)), s)
            o = o + jnp.dot(s.astype(BF16), v, preferred_element_type=F32)
            ku = (k * dec[c:2 * c, sl]).astype(BF16)
            st_ref[h] = st * dec[c - 1:c, sl] + _dot_tn(v, ku)
            ms = jnp.mean(o * o, axis=-1, keepdims=True)
            o = o * lax.rsqrt(ms + EPS) * gn * g_ref[rows, sl].astype(F32)
            o_ref[rows, sl] = o.astype(o_ref.dtype)
        return carry

    lax.fori_loop(0, t_blk // c, chunk, 0)


def _hgrn_recurrence(q, k, v, g, lf, gnorm, *, c=HGRN_CHUNK, t_blk=512, w=512):
    s, d = q.shape
    c = min(c, s)
    t_blk = min(t_blk, s)
    w = min(w, d)
    cm, lv, n_levels = _hgrn_constants(c)
    blk = pl.BlockSpec((t_blk, w), lambda hi, ti: (ti, hi))
    return pl.pallas_call(
        functools.partial(_hgrn_kernel, c=c, n_levels=n_levels),
        grid=(d // w, s // t_blk),
        in_specs=[blk, blk, blk, blk, blk,
                  pl.BlockSpec(cm.shape, lambda hi, ti: (0, 0)),
                  pl.BlockSpec(lv.shape, lambda hi, ti: (0, 0)),
                  pl.BlockSpec((1, HEAD_DIM), lambda hi, ti: (0, 0))],
        out_specs=blk,
        out_shape=jax.ShapeDtypeStruct((s, d), BF16),
        scratch_shapes=[pltpu.VMEM((w // HEAD_DIM, HEAD_DIM, HEAD_DIM), F32)],
        compiler_params=_params(("parallel", "arbitrary")),
        name="hgrn_recurrence",
    )(q, k, v, g, lf, jnp.asarray(cm, BF16), jnp.asarray(lv), gnorm.reshape(1, HEAD_DIM).astype(F32))


def _gmlp_kernel(z_ref, lng_ref, lnb_ref, wsp_ref, bsp_ref, o_ref, wc_ref):
    @pl.when(pl.program_id(0) == 0)
    def _():
        n = wsp_ref.shape[1]
        tri = lax.broadcasted_iota(jnp.int32, (n, n), 0) >= lax.broadcasted_iota(jnp.int32, (n, n), 1)
        for g in range(wsp_ref.shape[0]):
            wc_ref[g] = jnp.where(tri, wsp_ref[g], 0.0).astype(BF16)

    t_blk, d2 = z_ref.shape
    d = d2 // 2
    n_groups = d // HEAD_DIM
    for ci in range(t_blk // GMLP_CHUNK):
        rows = slice(ci * GMLP_CHUNK, (ci + 1) * GMLP_CHUNK)
        v = z_ref[rows, d:].astype(F32)
        mu = jnp.mean(v, axis=-1, keepdims=True)
        vc = v - mu
        var = jnp.mean(vc * vc, axis=-1, keepdims=True)
        rstd = lax.rsqrt(var + EPS)
        for g in range(n_groups):
            sl = slice(g * HEAD_DIM, (g + 1) * HEAD_DIM)
            vn = vc[:, sl] * rstd * lng_ref[:, sl] + lnb_ref[:, sl]
            mixed = jnp.dot(wc_ref[g], vn.astype(BF16), preferred_element_type=F32) + bsp_ref[g]
            o_ref[rows, sl] = (z_ref[rows, sl].astype(F32) * mixed).astype(o_ref.dtype)


def _gmlp_mix(z, ln_g, ln_b, w_sp, b_sp, *, t_blk=256):
    s, d2 = z.shape
    d = d2 // 2
    g, n, _ = w_sp.shape
    t_blk = min(t_blk, s)
    bsp = jnp.broadcast_to(b_sp.astype(F32)[:, :, None], (g, n, HEAD_DIM))
    return pl.pallas_call(
        _gmlp_kernel,
        grid=(s // t_blk,),
        in_specs=[pl.BlockSpec((t_blk, d2), lambda i: (i, 0)),
                  pl.BlockSpec((1, d), lambda i: (0, 0)),
                  pl.BlockSpec((1, d), lambda i: (0, 0)),
                  pl.BlockSpec((g, n, n), lambda i: (0, 0, 0)),
                  pl.BlockSpec((g, n, HEAD_DIM), lambda i: (0, 0, 0))],
        out_specs=pl.BlockSpec((t_blk, d), lambda i: (i, 0)),
        out_shape=jax.ShapeDtypeStruct((s, d), BF16),
        scratch_shapes=[pltpu.VMEM((g, n, n), BF16)],
        compiler_params=_params(("arbitrary",)),
        name="gmlp_mix",
    )(z, ln_g.reshape(1, d).astype(F32), ln_b.reshape(1, d).astype(F32), w_sp.astype(F32), bsp)


def kernel(x, norm_mix, norm_ffn, hgrn_w_in, hgrn_lower_bounds, hgrn_gnorm, hgrn_w_out, gmlp_w_in,
           gmlp_ln_gain, gmlp_ln_bias, gmlp_w_spatial, gmlp_b_spatial, gmlp_w_out, ffn_w1, ffn_w2, norm_final):
    bsz, s, d = x.shape
    assert bsz == 1, "the recurrence kernel carries one sequence"
    depth = norm_mix.shape[0]
    h = x.reshape(s, d).astype(F32)
    lower_bounds = _lower_bounds(hgrn_lower_bounds)
    hgrn_w_out_b = _cast_bf16(hgrn_w_out)
    gmlp_w_out_b = _cast_bf16(gmlp_w_out)
    ffn_w2_b = _cast_bf16(ffn_w2)
    hb, ss = _row_stats(h)

    for layer in range(depth):
        j = layer // N_MIXERS
        nw = norm_mix[layer]
        if layer % N_MIXERS == 0:
            lb = lower_bounds[layer].reshape(1, d)
            (q,) = _proj(hb, ss, hgrn_w_in, j, nw, n_off=0, n=d, epilogue=_ep_silu, out_dtypes=(BF16,),
                         name="hgrn_q")
            lf, k = _proj(hb, ss, hgrn_w_in, j, nw, n_off=d, n=d, epilogue=_ep_forget, extra_rows=(lb,),
                          out_dtypes=(F32, BF16), name="hgrn_f")
            (v,) = _proj(hb, ss, hgrn_w_in, j, nw, n_off=2 * d, n=d, epilogue=_ep_identity, out_dtypes=(BF16,),
                         name="hgrn_i")
            (g,) = _proj(hb, ss, hgrn_w_in, j, nw, n_off=3 * d, n=d, epilogue=_ep_silu, out_dtypes=(BF16,),
                         name="hgrn_g")
            mixed = _hgrn_recurrence(q, k, v, g, lf, hgrn_gnorm[j])
            w_out = hgrn_w_out_b
        else:
            (z,) = _proj(hb, ss, gmlp_w_in, j, nw, epilogue=_ep_gelu, out_dtypes=(BF16,), name="gmlp_in")
            mixed = _gmlp_mix(z, gmlp_ln_gain[j], gmlp_ln_bias[j], gmlp_w_spatial[j], gmlp_b_spatial[j])
            w_out = gmlp_w_out_b
        h, hb, ss = _out_proj(mixed, w_out, j, h, bn=512, name="mix_out")
        (hid,) = _proj(hb, ss, ffn_w1, layer, norm_ffn[layer], epilogue=_ep_relu2, out_dtypes=(BF16,),
                       name="ffn_up")
        h, hb, ss = _out_proj(hid, ffn_w2_b, layer, h, bk=2048, name="ffn_down")
    out = _rmsnorm(h, norm_final, F32)
    return out.reshape(bsz, s, d)
```

```python
import functools
import math

import numpy as np
import jax
import jax.numpy as jnp
from jax import lax
from jax.experimental import pallas as pl
from jax.experimental.pallas import tpu as pltpu

HEAD_DIM = 128
GMLP_CHUNK = 128
N_MIXERS = 2
EPS = 1e-5
HGRN_CHUNK = 128
SUBLANES = 8

V7X_VMEM_LIMIT_BYTES = 56 * 1024 * 1024

F32 = jnp.float32
BF16 = jnp.bfloat16


def _params(sem):
    return pltpu.CompilerParams(dimension_semantics=sem, vmem_limit_bytes=V7X_VMEM_LIMIT_BYTES)


def _sigmoid(x):
    return 1.0 / (1.0 + jnp.exp(-x))


def _lower_bounds_kernel(lb_ref, o_ref):
    x = lb_ref[...]
    m = jnp.max(x, axis=0, keepdims=True)
    e = jnp.exp(x - m)
    p = e / jnp.sum(e, axis=0, keepdims=True)
    depth = x.shape[0]
    run = p[0:1, :]
    rows = [run - p[0:1, :]]
    for l in range(1, depth):
        run = run + p[l:l + 1, :]
        rows.append(run - p[0:1, :])
    o_ref[...] = jnp.concatenate(rows, axis=0)


def _lower_bounds(lb):
    return pl.pallas_call(
        _lower_bounds_kernel,
        out_shape=jax.ShapeDtypeStruct(lb.shape, F32),
        name="lower_bounds",
    )(lb.astype(F32))


def _row_stats_kernel(x_ref, xb_ref, ss_ref):
    x = x_ref[...]
    xb_ref[...] = x.astype(xb_ref.dtype)
    ss_ref[...] = jnp.sum(x * x, axis=-1, keepdims=True)


def _row_stats(x):
    s, d = x.shape
    bm = min(256, s)
    return pl.pallas_call(
        _row_stats_kernel,
        grid=(s // bm,),
        in_specs=[pl.BlockSpec((bm, d), lambda i: (i, 0))],
        out_specs=(pl.BlockSpec((bm, d), lambda i: (i, 0)), pl.BlockSpec((bm, 1), lambda i: (i, 0))),
        out_shape=(jax.ShapeDtypeStruct((s, d), BF16), jax.ShapeDtypeStruct((s, 1), F32)),
        compiler_params=_params(("parallel",)),
        name="row_stats",
    )(x)


def _rmsnorm_kernel(x_ref, w_ref, o_ref):
    x = x_ref[...]
    ms = jnp.mean(x * x, axis=-1, keepdims=True)
    o_ref[...] = (x * lax.rsqrt(ms + EPS) * w_ref[...]).astype(o_ref.dtype)


def _rmsnorm(x, w, out_dtype):
    s, d = x.shape
    bm = min(256, s)
    return pl.pallas_call(
        _rmsnorm_kernel,
        grid=(s // bm,),
        in_specs=[pl.BlockSpec((bm, d), lambda i: (i, 0)),
                  pl.BlockSpec((1, d), lambda i: (0, 0))],
        out_specs=pl.BlockSpec((bm, d), lambda i: (i, 0)),
        out_shape=jax.ShapeDtypeStruct((s, d), out_dtype),
        compiler_params=_params(("parallel",)),
        name="rmsnorm",
    )(x, w.reshape(1, d).astype(F32))


def _ep_silu(acc):
    return (acc * _sigmoid(acc),)


def _ep_identity(acc):
    return (acc,)


def _ep_forget(acc, lb):
    forget = lb + (1.0 - lb) * _sigmoid(acc)
    return (jnp.log(forget), 1.0 - forget)


def _ep_gelu(acc):
    return (jax.nn.gelu(acc),)


def _ep_relu2(acc):
    r = jnp.maximum(acc, 0.0)
    return (r * r,)


def _proj_finish(acc, ss_ref, extra, outs, epilogue, inv_k):
    rstd = lax.rsqrt(ss_ref[...] * inv_k + EPS)
    res = epilogue(acc * rstd, *[e[...] for e in extra])
    for o_ref, r in zip(outs, res):
        o_ref[...] = r.astype(o_ref.dtype)


def _proj_head_kernel(a_ref, w_ref, nw_ref, ss_ref, *rest, epilogue, n_extra, inv_k):
    extra = rest[:n_extra]
    outs = rest[n_extra:-1]
    wb_ref = rest[-1]
    wb = (w_ref[...] * nw_ref[...]).astype(BF16)
    wb_ref[...] = wb
    acc = jnp.dot(a_ref[...], wb, preferred_element_type=F32)
    _proj_finish(acc, ss_ref, extra, outs, epilogue, inv_k)


def _proj_tail_kernel(a_ref, wb_ref, ss_ref, *rest, epilogue, n_extra, n_out, inv_k):
    extra = rest[:n_extra]
    outs = rest[n_extra + n_out:]
    acc = jnp.dot(a_ref[...], wb_ref[...], preferred_element_type=F32)
    _proj_finish(acc, ss_ref, extra, outs, epilogue, inv_k)


def _proj(hb, ss, w, layer, nw, *, n_off=0, n=None, epilogue, extra_rows=(), out_dtypes, bm=1024, bn=1024,
          bn_head=512, name):
    m, kdim = hb.shape
    n = w.shape[2] if n is None else n
    bm, bn, bnh = min(bm, m), min(bn, n), min(bn_head, n)
    n_extra, n_out = len(extra_rows), len(out_dtypes)
    out_shape = tuple(jax.ShapeDtypeStruct((m, n), dt) for dt in out_dtypes)
    off = n_off // bnh
    head = pl.pallas_call(
        functools.partial(_proj_head_kernel, epilogue=epilogue, n_extra=n_extra, inv_k=1.0 / kdim),
        grid=(n // bnh,),
        in_specs=[pl.BlockSpec((bm, kdim), lambda j: (0, 0)),
                  pl.BlockSpec((None, kdim, bnh), lambda j: (layer, 0, j + off)),
                  pl.BlockSpec((kdim, 1), lambda j: (0, 0)),
                  pl.BlockSpec((bm, 1), lambda j: (0, 0))]
                 + [pl.BlockSpec((1, bnh), lambda j: (0, j)) for _ in extra_rows],
        out_specs=tuple(pl.BlockSpec((bm, bnh), lambda j: (0, j)) for _ in out_dtypes)
                  + (pl.BlockSpec((kdim, bnh), lambda j: (0, j)),),
        out_shape=out_shape + (jax.ShapeDtypeStruct((kdim, n), BF16),),
        compiler_params=_params(("parallel",)),
        name=name + "_head",
    )(hb, w, nw.reshape(kdim, 1).astype(F32), ss, *extra_rows)
    outs, wb = head[:-1], head[-1]
    if m == bm:
        return outs
    n_in = 3 + n_extra
    return pl.pallas_call(
        functools.partial(_proj_tail_kernel, epilogue=epilogue, n_extra=n_extra, n_out=n_out, inv_k=1.0 / kdim),
        grid=(m // bm - 1, n // bn),
        in_specs=[pl.BlockSpec((bm, kdim), lambda i, j: (i + 1, 0)),
                  pl.BlockSpec((kdim, bn), lambda i, j: (0, j)),
                  pl.BlockSpec((bm, 1), lambda i, j: (i + 1, 0))]
                 + [pl.BlockSpec((1, bn), lambda i, j: (0, j)) for _ in extra_rows]
                 + [pl.BlockSpec(memory_space=pl.ANY) for _ in out_dtypes],
        out_specs=tuple(pl.BlockSpec((bm, bn), lambda i, j: (i + 1, j)) for _ in out_dtypes),
        out_shape=out_shape,
        input_output_aliases={n_in + o: o for o in range(n_out)},
        compiler_params=_params(("parallel", "parallel")),
        name=name,
    )(hb, wb, ss, *extra_rows, *outs)


def _emit_residual(acc, res_ref, h_ref, hb_ref, ss_ref, first_col_block):
    h = res_ref[...] + acc
    h_ref[...] = h
    hb_ref[...] = h.astype(hb_ref.dtype)
    part = jnp.sum(h * h, axis=-1, keepdims=True)

    @pl.when(first_col_block)
    def _():
        ss_ref[...] = part

    @pl.when(jnp.logical_not(first_col_block))
    def _():
        ss_ref[...] += part


def _accumulate(acc_ref, part, k, nk, emit):
    if nk == 1:
        emit(part)
        return

    @pl.when(k == 0)
    def _():
        acc_ref[...] = part

    @pl.when(jnp.logical_and(k > 0, k < nk - 1))
    def _():
        acc_ref[...] += part

    @pl.when(k == nk - 1)
    def _():
        emit(acc_ref[...] + part)


def _out_head_kernel(a_ref, w_ref, res_ref, h_ref, hb_ref, ss_ref, wb_ref, acc_ref, *, nk):
    wb = w_ref[...].astype(BF16)
    wb_ref[...] = wb
    part = jnp.dot(a_ref[...], wb, preferred_element_type=F32)
    first = pl.program_id(0) == 0
    _accumulate(acc_ref, part, pl.program_id(1), nk,
                lambda acc: _emit_residual(acc, res_ref, h_ref, hb_ref, ss_ref, first))


def _out_tail_kernel(a_ref, wb_ref, res_ref, h_alias, hb_alias, ss_alias, h_ref, hb_ref, ss_ref, acc_ref, *, nk):
    part = jnp.dot(a_ref[...], wb_ref[...], preferred_element_type=F32)
    first = pl.program_id(1) == 0
    _accumulate(acc_ref, part, pl.program_id(2), nk,
                lambda acc: _emit_residual(acc, res_ref, h_ref, hb_ref, ss_ref, first))


def _out_proj(a, w, layer, res, *, bm=1024, bn=1024, bk=2048, bn_head=512, name):
    m, kdim = a.shape
    n = w.shape[2]
    bm, bn, bnh, bk = min(bm, m), min(bn, n), min(bn_head, n), min(bk, kdim)
    nk = kdim // bk
    out_shape = (jax.ShapeDtypeStruct((m, n), F32), jax.ShapeDtypeStruct((m, n), BF16),
                 jax.ShapeDtypeStruct((m, 1), F32))
    h, hb, ss, wb = pl.pallas_call(
        functools.partial(_out_head_kernel, nk=nk),
        grid=(n // bnh, nk),
        in_specs=[pl.BlockSpec((bm, bk), lambda j, k: (0, k)),
                  pl.BlockSpec((None, bk, bnh), lambda j, k: (layer, k, j)),
                  pl.BlockSpec((bm, bnh), lambda j, k: (0, j))],
        out_specs=(pl.BlockSpec((bm, bnh), lambda j, k: (0, j)),
                   pl.BlockSpec((bm, bnh), lambda j, k: (0, j)),
                   pl.BlockSpec((bm, 1), lambda j, k: (0, 0)),
                   pl.BlockSpec((bk, bnh), lambda j, k: (k, j))),
        out_shape=out_shape + (jax.ShapeDtypeStruct((kdim, n), BF16),),
        scratch_shapes=[pltpu.VMEM((bm, bnh), F32)],
        compiler_params=_params(("arbitrary", "arbitrary")),
        name=name + "_head",
    )(a, w, res)
    if m == bm:
        return h, hb, ss
    blk = lambda i, j, k: (i + 1, j)
    return pl.pallas_call(
        functools.partial(_out_tail_kernel, nk=nk),
        grid=(m // bm - 1, n // bn, nk),
        in_specs=[pl.BlockSpec((bm, bk), lambda i, j, k: (i + 1, k)),
                  pl.BlockSpec((bk, bn), lambda i, j, k: (k, j)),
                  pl.BlockSpec((bm, bn), blk),
                  pl.BlockSpec(memory_space=pl.ANY), pl.BlockSpec(memory_space=pl.ANY),
                  pl.BlockSpec(memory_space=pl.ANY)],
        out_specs=(pl.BlockSpec((bm, bn), blk), pl.BlockSpec((bm, bn), blk),
                   pl.BlockSpec((bm, 1), lambda i, j, k: (i + 1, 0))),
        out_shape=out_shape,
        input_output_aliases={3: 0, 4: 1, 5: 2},
        scratch_shapes=[pltpu.VMEM((bm, bn), F32)],
        compiler_params=_params(("parallel", "arbitrary", "arbitrary")),
        name=name,
    )(a, wb, res, h, hb, ss)


MXU_LEVEL_HALF_SIZES = (4, 2)


def _hgrn_constants(c):
    n_levels = int(math.log2(c))
    t = np.arange(c)[:, None]
    j = np.arange(c)[None, :]
    blocks = [(j <= t)]
    level = np.full((c, c), -1, np.int32)
    for l in range(n_levels):
        m = c >> (l + 1)
        r = (t // (2 * m)) * (2 * m) + m - 1
        second = (t % (2 * m)) >= m
        if m in MXU_LEVEL_HALF_SIZES:
            blocks.append(np.where(second, (j > r) & (j <= t), (j > t) & (j <= r)))
        same = (t // (2 * m)) == (j // (2 * m))
        level[same & second & ((j % (2 * m)) < m)] = l
    level[np.arange(c), np.arange(c)] = n_levels
    cst = np.concatenate(blocks, axis=0).astype(np.float32)
    return np.concatenate([cst, cst], axis=1), level, n_levels


def _dot_nt(a, b):
    return lax.dot_general(a, b, (((1,), (1,)), ((), ())), preferred_element_type=F32)


def _dot_tn(a, b):
    return lax.dot_general(a, b, (((0,), (0,)), ((), ())), preferred_element_type=F32)


def _level_exponents(p, p_ref, lf, ex, c, n_levels):
    out = []
    for l in range(n_levels):
        m = c >> (l + 1)
        if m >= SUBLANES:
            pieces = []
            for b in range(c // (2 * m)):
                lo_row = 2 * m * b
                pr = p_ref[lo_row + m - 1:lo_row + m, :]
                pieces.append(pr - p[lo_row:lo_row + m])
                pieces.append(p[lo_row + m:lo_row + 2 * m] - pr)
            out.append(jnp.concatenate(pieces, axis=0))
        elif m in MXU_LEVEL_HALF_SIZES:
            i = 1 + MXU_LEVEL_HALF_SIZES.index(m)
            out.append(ex[i * c:(i + 1) * c])
        else:
            odd = (lax.broadcasted_iota(jnp.int32, lf.shape, 0) & 1) == 1
            out.append(jnp.where(odd, lf, 0.0))
    return out


def _hgrn_kernel(q_ref, k_ref, v_ref, g_ref, lf_ref, cm_ref, lv_ref, gn_ref, o_ref, st_ref, p_ref, *, c, n_levels):
    @pl.when(pl.program_id(1) == 0)
    def _():
        st_ref[...] = jnp.zeros_like(st_ref)

    t_blk, w = q_ref.shape
    n_heads = w // HEAD_DIM
    lv = lv_ref[...]
    gn = gn_ref[...]

    def chunk(ci, carry):
        r0 = pl.multiple_of(ci * c, c)
        rows = pl.ds(r0, c)
        lf = lf_ref[rows, :]
        hi = lf.astype(BF16)
        lo = (lf - hi.astype(F32)).astype(BF16)
        ex = jnp.dot(cm_ref[...], jnp.concatenate([hi, lo], axis=0), preferred_element_type=F32)
        p = ex[0:c]
        p_ref[...] = p
        dec_q = jnp.exp(p)
        p_last = p_ref[c - 1:c, :]
        dec_k = jnp.exp(p_last - p)
        dec_end = jnp.exp(p_last)
        dec_l = [jnp.exp(e) for e in _level_exponents(p, p_ref, lf, ex, c, n_levels)]
        for h in range(n_heads):
            sl = slice(h * HEAD_DIM, (h + 1) * HEAD_DIM)
            q = q_ref[rows, sl].astype(F32)
            k = k_ref[rows, sl].astype(F32)
            v = v_ref[rows, sl]
            st = st_ref[h]
            o = _dot_nt((q * dec_q[:, sl]).astype(BF16), st.astype(BF16))
            s = jnp.where(lv == n_levels, _dot_nt(q.astype(BF16), k.astype(BF16)), 0.0)
            for l in range(n_levels):
                d = dec_l[l][:, sl]
                s = jnp.where(lv == l, _dot_nt((q * d).astype(BF16), (k * d).astype(BF16)), s)
            o = o + jnp.dot(s.astype(BF16), v, preferred_element_type=F32)
            ku = (k * dec_k[:, sl]).astype(BF16)
            st_ref[h] = st * dec_end[:, sl] + _dot_tn(v, ku)
            ms = jnp.mean(o * o, axis=-1, keepdims=True)
            o = o * lax.rsqrt(ms + EPS) * gn * g_ref[rows, sl].astype(F32)
            o_ref[rows, sl] = o.astype(o_ref.dtype)
        return carry

    lax.fori_loop(0, t_blk // c, chunk, 0)


def _hgrn_recurrence(q, k, v, g, lf, gnorm, *, c=HGRN_CHUNK, t_blk=512, w=512):
    s, d = q.shape
    c = min(c, s)
    assert c >= 2 * SUBLANES and c & (c - 1) == 0, "chunk must be a power of two of at least two sublane tiles"
    t_blk = min(t_blk, s)
    w = min(w, d)
    cm, lv, n_levels = _hgrn_constants(c)
    blk = pl.BlockSpec((t_blk, w), lambda hi, ti: (ti, hi))
    return pl.pallas_call(
        functools.partial(_hgrn_kernel, c=c, n_levels=n_levels),
        grid=(d // w, s // t_blk),
        in_specs=[blk, blk, blk, blk, blk,
                  pl.BlockSpec(cm.shape, lambda hi, ti: (0, 0)),
                  pl.BlockSpec(lv.shape, lambda hi, ti: (0, 0)),
                  pl.BlockSpec((1, HEAD_DIM), lambda hi, ti: (0, 0))],
        out_specs=blk,
        out_shape=jax.ShapeDtypeStruct((s, d), BF16),
        scratch_shapes=[pltpu.VMEM((w // HEAD_DIM, HEAD_DIM, HEAD_DIM), F32), pltpu.VMEM((c, w), F32)],
        compiler_params=_params(("parallel", "arbitrary")),
        name="hgrn_recurrence",
    )(q, k, v, g, lf, jnp.asarray(cm, BF16), jnp.asarray(lv), gnorm.reshape(1, HEAD_DIM).astype(F32))


def _gmlp_kernel(z_ref, lng_ref, lnb_ref, wsp_ref, bsp_ref, o_ref, wc_ref):
    @pl.when(pl.program_id(0) == 0)
    def _():
        n = wsp_ref.shape[1]
        tri = lax.broadcasted_iota(jnp.int32, (n, n), 0) >= lax.broadcasted_iota(jnp.int32, (n, n), 1)
        for g in range(wsp_ref.shape[0]):
            wc_ref[g] = jnp.where(tri, wsp_ref[g], 0.0).astype(BF16)

    t_blk, d2 = z_ref.shape
    d = d2 // 2
    n_groups = d // HEAD_DIM
    for ci in range(t_blk // GMLP_CHUNK):
        rows = slice(ci * GMLP_CHUNK, (ci + 1) * GMLP_CHUNK)
        v = z_ref[rows, d:].astype(F32)
        mu = jnp.mean(v, axis=-1, keepdims=True)
        vc = v - mu
        var = jnp.mean(vc * vc, axis=-1, keepdims=True)
        rstd = lax.rsqrt(var + EPS)
        for g in range(n_groups):
            sl = slice(g * HEAD_DIM, (g + 1) * HEAD_DIM)
            vn = vc[:, sl] * rstd * lng_ref[:, sl] + lnb_ref[:, sl]
            mixed = jnp.dot(wc_ref[g], vn.astype(BF16), preferred_element_type=F32) + bsp_ref[g]
            o_ref[rows, sl] = (z_ref[rows, sl].astype(F32) * mixed).astype(o_ref.dtype)


def _gmlp_mix(z, ln_g, ln_b, w_sp, b_sp, *, t_blk=256):
    s, d2 = z.shape
    d = d2 // 2
    g, n, _ = w_sp.shape
    t_blk = min(t_blk, s)
    bsp = jnp.broadcast_to(b_sp.astype(F32)[:, :, None], (g, n, HEAD_DIM))
    return pl.pallas_call(
        _gmlp_kernel,
        grid=(s // t_blk,),
        in_specs=[pl.BlockSpec((t_blk, d2), lambda i: (i, 0)),
                  pl.BlockSpec((1, d), lambda i: (0, 0)),
                  pl.BlockSpec((1, d), lambda i: (0, 0)),
                  pl.BlockSpec((g, n, n), lambda i: (0, 0, 0)),
                  pl.BlockSpec((g, n, HEAD_DIM), lambda i: (0, 0, 0))],
        out_specs=pl.BlockSpec((t_blk, d), lambda i: (i, 0)),
        out_shape=jax.ShapeDtypeStruct((s, d), BF16),
        scratch_shapes=[pltpu.VMEM((g, n, n), BF16)],
        compiler_params=_params(("arbitrary",)),
        name="gmlp_mix",
    )(z, ln_g.reshape(1, d).astype(F32), ln_b.reshape(1, d).astype(F32), w_sp.astype(F32), bsp)


def kernel(x, norm_mix, norm_ffn, hgrn_w_in, hgrn_lower_bounds, hgrn_gnorm, hgrn_w_out, gmlp_w_in,
           gmlp_ln_gain, gmlp_ln_bias, gmlp_w_spatial, gmlp_b_spatial, gmlp_w_out, ffn_w1, ffn_w2, norm_final):
    bsz, s, d = x.shape
    assert bsz == 1, "the recurrence kernel carries one sequence"
    depth = norm_mix.shape[0]
    h = x.reshape(s, d).astype(F32)
    lower_bounds = _lower_bounds(hgrn_lower_bounds)
    hb, ss = _row_stats(h)

    for layer in range(depth):
        j = layer // N_MIXERS
        nw = norm_mix[layer]
        if layer % N_MIXERS == 0:
            lb = lower_bounds[layer].reshape(1, d)
            (q,) = _proj(hb, ss, hgrn_w_in, j, nw, n_off=0, n=d, epilogue=_ep_silu, out_dtypes=(BF16,),
                         name="hgrn_q")
            lf, k = _proj(hb, ss, hgrn_w_in, j, nw, n_off=d, n=d, epilogue=_ep_forget, extra_rows=(lb,),
                          out_dtypes=(F32, BF16), name="hgrn_f")
            (v,) = _proj(hb, ss, hgrn_w_in, j, nw, n_off=2 * d, n=d, epilogue=_ep_identity, out_dtypes=(BF16,),
                         name="hgrn_i")
            (g,) = _proj(hb, ss, hgrn_w_in, j, nw, n_off=3 * d, n=d, epilogue=_ep_silu, out_dtypes=(BF16,),
                         name="hgrn_g")
            mixed = _hgrn_recurrence(q, k, v, g, lf, hgrn_gnorm[j])
            w_out = hgrn_w_out
        else:
            (z,) = _proj(hb, ss, gmlp_w_in, j, nw, epilogue=_ep_gelu, out_dtypes=(BF16,), name="gmlp_in")
            mixed = _gmlp_mix(z, gmlp_ln_gain[j], gmlp_ln_bias[j], gmlp_w_spatial[j], gmlp_b_spatial[j])
            w_out = gmlp_w_out
        h, hb, ss = _out_proj(mixed, w_out, j, h, name="mix_out")
        (hid,) = _proj(hb, ss, ffn_w1, layer, norm_ffn[layer], epilogue=_ep_relu2, out_dtypes=(BF16,),
                       name="ffn_up")
        h, hb, ss = _out_proj(hid, ffn_w2, layer, h, name="ffn_down")
    out = _rmsnorm(h, norm_final, F32)
    return out.reshape(bsz, s, d)
```

```python
import functools
import math

import numpy as np
import jax
import jax.numpy as jnp
from jax import lax
from jax.experimental import pallas as pl
from jax.experimental.pallas import tpu as pltpu

HEAD_DIM = 128
GMLP_CHUNK = 128
N_MIXERS = 2
EPS = 1e-5
HGRN_CHUNK = 128
SUBLANES = 8

V7X_VMEM_LIMIT_BYTES = 56 * 1024 * 1024

F32 = jnp.float32
BF16 = jnp.bfloat16


def _params(sem):
    return pltpu.CompilerParams(dimension_semantics=sem, vmem_limit_bytes=V7X_VMEM_LIMIT_BYTES)


def _sigmoid(x):
    return 1.0 / (1.0 + jnp.exp(-x))


def _lower_bounds_kernel(lb_ref, o_ref):
    x = lb_ref[...]
    m = jnp.max(x, axis=0, keepdims=True)
    e = jnp.exp(x - m)
    p = e / jnp.sum(e, axis=0, keepdims=True)
    depth = x.shape[0]
    run = p[0:1, :]
    rows = [run - p[0:1, :]]
    for l in range(1, depth):
        run = run + p[l:l + 1, :]
        rows.append(run - p[0:1, :])
    o_ref[...] = jnp.concatenate(rows, axis=0)


def _lower_bounds(lb):
    return pl.pallas_call(
        _lower_bounds_kernel,
        out_shape=jax.ShapeDtypeStruct(lb.shape, F32),
        name="lower_bounds",
    )(lb.astype(F32))


def _row_stats_kernel(x_ref, xb_ref, ss_ref):
    x = x_ref[...]
    xb_ref[...] = x.astype(xb_ref.dtype)
    ss_ref[...] = jnp.sum(x * x, axis=-1, keepdims=True)


def _row_stats(x):
    s, d = x.shape
    bm = min(256, s)
    return pl.pallas_call(
        _row_stats_kernel,
        grid=(s // bm,),
        in_specs=[pl.BlockSpec((bm, d), lambda i: (i, 0))],
        out_specs=(pl.BlockSpec((bm, d), lambda i: (i, 0)), pl.BlockSpec((bm, 1), lambda i: (i, 0))),
        out_shape=(jax.ShapeDtypeStruct((s, d), BF16), jax.ShapeDtypeStruct((s, 1), F32)),
        compiler_params=_params(("parallel",)),
        name="row_stats",
    )(x)


def _rmsnorm_kernel(x_ref, w_ref, o_ref):
    x = x_ref[...]
    ms = jnp.mean(x * x, axis=-1, keepdims=True)
    o_ref[...] = (x * lax.rsqrt(ms + EPS) * w_ref[...]).astype(o_ref.dtype)


def _rmsnorm(x, w, out_dtype):
    s, d = x.shape
    bm = min(256, s)
    return pl.pallas_call(
        _rmsnorm_kernel,
        grid=(s // bm,),
        in_specs=[pl.BlockSpec((bm, d), lambda i: (i, 0)),
                  pl.BlockSpec((1, d), lambda i: (0, 0))],
        out_specs=pl.BlockSpec((bm, d), lambda i: (i, 0)),
        out_shape=jax.ShapeDtypeStruct((s, d), out_dtype),
        compiler_params=_params(("parallel",)),
        name="rmsnorm",
    )(x, w.reshape(1, d).astype(F32))


def _ep_silu(acc):
    return (acc * _sigmoid(acc),)


def _ep_identity(acc):
    return (acc,)


def _ep_forget(acc, lb):
    forget = lb + (1.0 - lb) * _sigmoid(acc)
    return (jnp.log(forget), 1.0 - forget)


def _ep_gelu(acc):
    return (jax.nn.gelu(acc),)


def _ep_relu2(acc):
    r = jnp.maximum(acc, 0.0)
    return (r * r,)


def _proj_finish(acc, ss_ref, extra, outs, epilogue, inv_k):
    rstd = lax.rsqrt(ss_ref[...] * inv_k + EPS)
    res = epilogue(acc * rstd, *[e[...] for e in extra])
    for o_ref, r in zip(outs, res):
        o_ref[...] = r.astype(o_ref.dtype)


def _proj_head_kernel(a_ref, w_ref, nw_ref, ss_ref, *rest, epilogue, n_extra, inv_k):
    extra = rest[:n_extra]
    outs = rest[n_extra:-1]
    wb_ref = rest[-1]
    wb = (w_ref[...] * nw_ref[...]).astype(BF16)
    wb_ref[...] = wb
    acc = jnp.dot(a_ref[...], wb, preferred_element_type=F32)
    _proj_finish(acc, ss_ref, extra, outs, epilogue, inv_k)


def _proj_tail_kernel(a_ref, wb_ref, ss_ref, *rest, epilogue, n_extra, n_out, inv_k):
    extra = rest[:n_extra]
    outs = rest[n_extra + n_out:]
    acc = jnp.dot(a_ref[...], wb_ref[...], preferred_element_type=F32)
    _proj_finish(acc, ss_ref, extra, outs, epilogue, inv_k)


def _proj(hb, ss, w, layer, nw, *, n_off=0, n=None, epilogue, extra_rows=(), out_dtypes, bm=1024, bn=1024,
          bn_head=512, name):
    m, kdim = hb.shape
    n = w.shape[2] if n is None else n
    bm, bn, bnh = min(bm, m), min(bn, n), min(bn_head, n)
    n_extra, n_out = len(extra_rows), len(out_dtypes)
    out_shape = tuple(jax.ShapeDtypeStruct((m, n), dt) for dt in out_dtypes)
    off = n_off // bnh
    head = pl.pallas_call(
        functools.partial(_proj_head_kernel, epilogue=epilogue, n_extra=n_extra, inv_k=1.0 / kdim),
        grid=(n // bnh,),
        in_specs=[pl.BlockSpec((bm, kdim), lambda j: (0, 0)),
                  pl.BlockSpec((None, kdim, bnh), lambda j: (layer, 0, j + off)),
                  pl.BlockSpec((kdim, 1), lambda j: (0, 0)),
                  pl.BlockSpec((bm, 1), lambda j: (0, 0))]
                 + [pl.BlockSpec((1, bnh), lambda j: (0, j)) for _ in extra_rows],
        out_specs=tuple(pl.BlockSpec((bm, bnh), lambda j: (0, j)) for _ in out_dtypes)
                  + (pl.BlockSpec((kdim, bnh), lambda j: (0, j)),),
        out_shape=out_shape + (jax.ShapeDtypeStruct((kdim, n), BF16),),
        compiler_params=_params(("parallel",)),
        name=name + "_head",
    )(hb, w, nw.reshape(kdim, 1).astype(F32), ss, *extra_rows)
    outs, wb = head[:-1], head[-1]
    if m == bm:
        return outs
    n_in = 3 + n_extra
    return pl.pallas_call(
        functools.partial(_proj_tail_kernel, epilogue=epilogue, n_extra=n_extra, n_out=n_out, inv_k=1.0 / kdim),
        grid=(m // bm - 1, n // bn),
        in_specs=[pl.BlockSpec((bm, kdim), lambda i, j: (i + 1, 0)),
                  pl.BlockSpec((kdim, bn), lambda i, j: (0, j)),
                  pl.BlockSpec((bm, 1), lambda i, j: (i + 1, 0))]
                 + [pl.BlockSpec((1, bn), lambda i, j: (0, j)) for _ in extra_rows]
                 + [pl.BlockSpec(memory_space=pl.ANY) for _ in out_dtypes],
        out_specs=tuple(pl.BlockSpec((bm, bn), lambda i, j: (i + 1, j)) for _ in out_dtypes),
        out_shape=out_shape,
        input_output_aliases={n_in + o: o for o in range(n_out)},
        compiler_params=_params(("parallel", "parallel")),
        name=name,
    )(hb, wb, ss, *extra_rows, *outs)


def _emit_residual(acc, res_ref, h_ref, hb_ref, ss_ref, first_col_block):
    h = res_ref[...] + acc
    h_ref[...] = h
    hb_ref[...] = h.astype(hb_ref.dtype)
    part = jnp.sum(h * h, axis=-1, keepdims=True)

    @pl.when(first_col_block)
    def _():
        ss_ref[...] = part

    @pl.when(jnp.logical_not(first_col_block))
    def _():
        ss_ref[...] += part


def _accumulate(acc_ref, partial_product, k, nk, emit):
    if nk == 1:
        emit(partial_product())
        return

    @pl.when(k == 0)
    def _():
        acc_ref[...] = partial_product()

    @pl.when(jnp.logical_and(k > 0, k < nk - 1))
    def _():
        acc_ref[...] += partial_product()

    @pl.when(k == nk - 1)
    def _():
        emit(acc_ref[...] + partial_product())


def _out_head_kernel(a_ref, w_ref, res_ref, h_ref, hb_ref, ss_ref, wb_ref, acc_ref, *, nk):
    def partial_product():
        wb = w_ref[...].astype(BF16)
        wb_ref[...] = wb
        return jnp.dot(a_ref[...], wb, preferred_element_type=F32)

    first = pl.program_id(0) == 0
    _accumulate(acc_ref, partial_product, pl.program_id(1), nk,
                lambda acc: _emit_residual(acc, res_ref, h_ref, hb_ref, ss_ref, first))


def _out_tail_kernel(a_ref, wb_ref, res_ref, h_alias, hb_alias, ss_alias, h_ref, hb_ref, ss_ref, acc_ref, *, nk):
    first = pl.program_id(1) == 0
    _accumulate(acc_ref, lambda: jnp.dot(a_ref[...], wb_ref[...], preferred_element_type=F32),
                pl.program_id(2), nk, lambda acc: _emit_residual(acc, res_ref, h_ref, hb_ref, ss_ref, first))


def _out_proj(a, w, layer, res, *, bm=1024, bn=1024, bk=2048, bn_head=512, bk_head=2048, name):
    m, kdim = a.shape
    n = w.shape[2]
    bm, bn, bnh, bk, bkh = min(bm, m), min(bn, n), min(bn_head, n), min(bk, kdim), min(bk_head, kdim)
    nk, nkh = kdim // bk, kdim // bkh
    out_shape = (jax.ShapeDtypeStruct((m, n), F32), jax.ShapeDtypeStruct((m, n), BF16),
                 jax.ShapeDtypeStruct((m, 1), F32))
    h, hb, ss, wb = pl.pallas_call(
        functools.partial(_out_head_kernel, nk=nkh),
        grid=(n // bnh, nkh),
        in_specs=[pl.BlockSpec((bm, bkh), lambda j, k: (0, k)),
                  pl.BlockSpec((None, bkh, bnh), lambda j, k: (layer, k, j)),
                  pl.BlockSpec((bm, bnh), lambda j, k: (0, j))],
        out_specs=(pl.BlockSpec((bm, bnh), lambda j, k: (0, j)),
                   pl.BlockSpec((bm, bnh), lambda j, k: (0, j)),
                   pl.BlockSpec((bm, 1), lambda j, k: (0, 0)),
                   pl.BlockSpec((bkh, bnh), lambda j, k: (k, j))),
        out_shape=out_shape + (jax.ShapeDtypeStruct((kdim, n), BF16),),
        scratch_shapes=[pltpu.VMEM((bm, bnh), F32)],
        compiler_params=_params(("arbitrary", "arbitrary")),
        name=name + "_head",
    )(a, w, res)
    if m == bm:
        return h, hb, ss
    blk = lambda i, j, k: (i + 1, j)
    return pl.pallas_call(
        functools.partial(_out_tail_kernel, nk=nk),
        grid=(m // bm - 1, n // bn, nk),
        in_specs=[pl.BlockSpec((bm, bk), lambda i, j, k: (i + 1, k)),
                  pl.BlockSpec((bk, bn), lambda i, j, k: (k, j)),
                  pl.BlockSpec((bm, bn), blk),
                  pl.BlockSpec(memory_space=pl.ANY), pl.BlockSpec(memory_space=pl.ANY),
                  pl.BlockSpec(memory_space=pl.ANY)],
        out_specs=(pl.BlockSpec((bm, bn), blk), pl.BlockSpec((bm, bn), blk),
                   pl.BlockSpec((bm, 1), lambda i, j, k: (i + 1, 0))),
        out_shape=out_shape,
        input_output_aliases={3: 0, 4: 1, 5: 2},
        scratch_shapes=[pltpu.VMEM((bm, bn), F32)],
        compiler_params=_params(("parallel", "arbitrary", "arbitrary")),
        name=name,
    )(a, wb, res, h, hb, ss)


MXU_LEVEL_HALF_SIZES = (4, 2)


def _hgrn_constants(c):
    n_levels = int(math.log2(c))
    t = np.arange(c)[:, None]
    j = np.arange(c)[None, :]
    blocks = [(j <= t)]
    level = np.full((c, c), -1, np.int32)
    for l in range(n_levels):
        m = c >> (l + 1)
        r = (t // (2 * m)) * (2 * m) + m - 1
        second = (t % (2 * m)) >= m
        if m in MXU_LEVEL_HALF_SIZES:
            blocks.append(np.where(second, (j > r) & (j <= t), (j > t) & (j <= r)))
        same = (t // (2 * m)) == (j // (2 * m))
        level[same & second & ((j % (2 * m)) < m)] = l
    level[np.arange(c), np.arange(c)] = n_levels
    cst = np.concatenate(blocks, axis=0).astype(np.float32)
    return np.concatenate([cst, cst], axis=1), level, n_levels


def _dot_nt(a, b):
    return lax.dot_general(a, b, (((1,), (1,)), ((), ())), preferred_element_type=F32)


def _dot_tn(a, b):
    return lax.dot_general(a, b, (((0,), (0,)), ((), ())), preferred_element_type=F32)


def _level_exponents(p, p_ref, lf, ex, c, n_levels):
    out = []
    for l in range(n_levels):
        m = c >> (l + 1)
        if m >= SUBLANES:
            pieces = []
            for b in range(c // (2 * m)):
                lo_row = 2 * m * b
                pr = p_ref[lo_row + m - 1:lo_row + m, :]
                pieces.append(pr - p[lo_row:lo_row + m])
                pieces.append(p[lo_row + m:lo_row + 2 * m] - pr)
            out.append(jnp.concatenate(pieces, axis=0))
        elif m in MXU_LEVEL_HALF_SIZES:
            i = 1 + MXU_LEVEL_HALF_SIZES.index(m)
            out.append(ex[i * c:(i + 1) * c])
        else:
            odd = (lax.broadcasted_iota(jnp.int32, lf.shape, 0) & 1) == 1
            out.append(jnp.where(odd, lf, 0.0))
    return out


def _hgrn_kernel(q_ref, k_ref, v_ref, g_ref, lf_ref, cm_ref, lv_ref, gn_ref, o_ref, st_ref, p_ref, *, c, n_levels):
    @pl.when(pl.program_id(1) == 0)
    def _():
        st_ref[...] = jnp.zeros_like(st_ref)

    t_blk, w = q_ref.shape
    n_heads = w // HEAD_DIM
    lv = lv_ref[...]
    gn = gn_ref[...]

    def chunk(ci, carry):
        r0 = pl.multiple_of(ci * c, c)
        rows = pl.ds(r0, c)
        lf = lf_ref[rows, :]
        hi = lf.astype(BF16)
        lo = (lf - hi.astype(F32)).astype(BF16)
        ex = jnp.dot(cm_ref[...], jnp.concatenate([hi, lo], axis=0), preferred_element_type=F32)
        p = ex[0:c]
        p_ref[...] = p
        dec_q = jnp.exp(p)
        p_last = p_ref[c - 1:c, :]
        dec_k = jnp.exp(p_last - p)
        dec_end = jnp.exp(p_last)
        dec_l = [jnp.exp(e) for e in _level_exponents(p, p_ref, lf, ex, c, n_levels)]
        for h in range(n_heads):
            sl = slice(h * HEAD_DIM, (h + 1) * HEAD_DIM)
            q = q_ref[rows, sl].astype(F32)
            k = k_ref[rows, sl].astype(F32)
            v = v_ref[rows, sl]
            st = st_ref[h]
            o = _dot_nt((q * dec_q[:, sl]).astype(BF16), st.astype(BF16))
            s = jnp.where(lv == n_levels, jnp.dot(q.astype(BF16), k.T.astype(BF16), preferred_element_type=F32), 0.0)
            for l in range(n_levels):
                d = dec_l[l][:, sl]
                s = jnp.where(lv == l, jnp.dot((q * d).astype(BF16), (k * d).T.astype(BF16),
                                               preferred_element_type=F32), s)
            o = o + jnp.dot(s.astype(BF16), v, preferred_element_type=F32)
            ku = (k * dec_k[:, sl]).astype(BF16)
            st_ref[h] = st * dec_end[:, sl] + _dot_tn(v, ku)
            ms = jnp.mean(o * o, axis=-1, keepdims=True)
            o = o * lax.rsqrt(ms + EPS) * gn * g_ref[rows, sl].astype(F32)
            o_ref[rows, sl] = o.astype(o_ref.dtype)
        return carry

    lax.fori_loop(0, t_blk // c, chunk, 0, unroll=True)


def _hgrn_recurrence(q, k, v, g, lf, gnorm, *, c=HGRN_CHUNK, t_blk=512, w=512):
    s, d = q.shape
    c = min(c, s)
    assert c >= 2 * SUBLANES and c & (c - 1) == 0, "chunk must be a power of two of at least two sublane tiles"
    t_blk = min(t_blk, s)
    w = min(w, d)
    cm, lv, n_levels = _hgrn_constants(c)
    blk = pl.BlockSpec((t_blk, w), lambda hi, ti: (ti, hi))
    return pl.pallas_call(
        functools.partial(_hgrn_kernel, c=c, n_levels=n_levels),
        grid=(d // w, s // t_blk),
        in_specs=[blk, blk, blk, blk, blk,
                  pl.BlockSpec(cm.shape, lambda hi, ti: (0, 0)),
                  pl.BlockSpec(lv.shape, lambda hi, ti: (0, 0)),
                  pl.BlockSpec((1, HEAD_DIM), lambda hi, ti: (0, 0))],
        out_specs=blk,
        out_shape=jax.ShapeDtypeStruct((s, d), BF16),
        scratch_shapes=[pltpu.VMEM((w // HEAD_DIM, HEAD_DIM, HEAD_DIM), F32), pltpu.VMEM((c, w), F32)],
        compiler_params=_params(("parallel", "arbitrary")),
        name="hgrn_recurrence",
    )(q, k, v, g, lf, jnp.asarray(cm, BF16), jnp.asarray(lv), gnorm.reshape(1, HEAD_DIM).astype(F32))


def _gmlp_kernel(z_ref, lng_ref, lnb_ref, wsp_ref, bsp_ref, o_ref, wc_ref):
    @pl.when(pl.program_id(0) == 0)
    def _():
        n = wsp_ref.shape[1]
        tri = lax.broadcasted_iota(jnp.int32, (n, n), 0) >= lax.broadcasted_iota(jnp.int32, (n, n), 1)
        for g in range(wsp_ref.shape[0]):
            wc_ref[g] = jnp.where(tri, wsp_ref[g], 0.0).astype(BF16)

    t_blk, d2 = z_ref.shape
    d = d2 // 2
    n_groups = d // HEAD_DIM
    for ci in range(t_blk // GMLP_CHUNK):
        rows = slice(ci * GMLP_CHUNK, (ci + 1) * GMLP_CHUNK)
        v = z_ref[rows, d:].astype(F32)
        mu = jnp.mean(v, axis=-1, keepdims=True)
        vc = v - mu
        var = jnp.mean(vc * vc, axis=-1, keepdims=True)
        rstd = lax.rsqrt(var + EPS)
        for g in range(n_groups):
            sl = slice(g * HEAD_DIM, (g + 1) * HEAD_DIM)
            vn = vc[:, sl] * rstd * lng_ref[:, sl] + lnb_ref[:, sl]
            mixed = jnp.dot(wc_ref[g], vn.astype(BF16), preferred_element_type=F32) + bsp_ref[g]
            o_ref[rows, sl] = (z_ref[rows, sl].astype(F32) * mixed).astype(o_ref.dtype)


def _gmlp_mix(z, ln_g, ln_b, w_sp, b_sp, *, t_blk=256):
    s, d2 = z.shape
    d = d2 // 2
    g, n, _ = w_sp.shape
    t_blk = min(t_blk, s)
    bsp = jnp.broadcast_to(b_sp.astype(F32)[:, :, None], (g, n, HEAD_DIM))
    return pl.pallas_call(
        _gmlp_kernel,
        grid=(s // t_blk,),
        in_specs=[pl.BlockSpec((t_blk, d2), lambda i: (i, 0)),
                  pl.BlockSpec((1, d), lambda i: (0, 0)),
                  pl.BlockSpec((1, d), lambda i: (0, 0)),
                  pl.BlockSpec((g, n, n), lambda i: (0, 0, 0)),
                  pl.BlockSpec((g, n, HEAD_DIM), lambda i: (0, 0, 0))],
        out_specs=pl.BlockSpec((t_blk, d), lambda i: (i, 0)),
        out_shape=jax.ShapeDtypeStruct((s, d), BF16),
        scratch_shapes=[pltpu.VMEM((g, n, n), BF16)],
        compiler_params=_params(("arbitrary",)),
        name="gmlp_mix",
    )(z, ln_g.reshape(1, d).astype(F32), ln_b.reshape(1, d).astype(F32), w_sp.astype(F32), bsp)


def kernel(x, norm_mix, norm_ffn, hgrn_w_in, hgrn_lower_bounds, hgrn_gnorm, hgrn_w_out, gmlp_w_in,
           gmlp_ln_gain, gmlp_ln_bias, gmlp_w_spatial, gmlp_b_spatial, gmlp_w_out, ffn_w1, ffn_w2, norm_final):
    bsz, s, d = x.shape
    assert bsz == 1, "the recurrence kernel carries one sequence"
    depth = norm_mix.shape[0]
    h = x.reshape(s, d).astype(F32)
    lower_bounds = _lower_bounds(hgrn_lower_bounds)
    hb, ss = _row_stats(h)

    for layer in range(depth):
        j = layer // N_MIXERS
        nw = norm_mix[layer]
        if layer % N_MIXERS == 0:
            lb = lower_bounds[layer].reshape(1, d)
            (q,) = _proj(hb, ss, hgrn_w_in, j, nw, n_off=0, n=d, epilogue=_ep_silu, out_dtypes=(BF16,),
                         name="hgrn_q")
            lf, k = _proj(hb, ss, hgrn_w_in, j, nw, n_off=d, n=d, epilogue=_ep_forget, extra_rows=(lb,),
                          out_dtypes=(F32, BF16), name="hgrn_f")
            (v,) = _proj(hb, ss, hgrn_w_in, j, nw, n_off=2 * d, n=d, epilogue=_ep_identity, out_dtypes=(BF16,),
                         name="hgrn_i")
            (g,) = _proj(hb, ss, hgrn_w_in, j, nw, n_off=3 * d, n=d, epilogue=_ep_silu, out_dtypes=(BF16,),
                         name="hgrn_g")
            mixed = _hgrn_recurrence(q, k, v, g, lf, hgrn_gnorm[j])
            w_out = hgrn_w_out
        else:
            (z,) = _proj(hb, ss, gmlp_w_in, j, nw, epilogue=_ep_gelu, out_dtypes=(BF16,), name="gmlp_in")
            mixed = _gmlp_mix(z, gmlp_ln_gain[j], gmlp_ln_bias[j], gmlp_w_spatial[j], gmlp_b_spatial[j])
            w_out = gmlp_w_out
        h, hb, ss = _out_proj(mixed, w_out, j, h, bn=512, bk=d, name="mix_out")
        (hid,) = _proj(hb, ss, ffn_w1, layer, norm_ffn[layer], epilogue=_ep_relu2, out_dtypes=(BF16,),
                       name="ffn_up")
        h, hb, ss = _out_proj(hid, ffn_w2, layer, h, name="ffn_down")
    out = _rmsnorm(h, norm_final, F32)
    return out.reshape(bsz, s, d)
```

```python
import functools
import math

import numpy as np
import jax
import jax.numpy as jnp
from jax import lax
from jax.experimental import pallas as pl
from jax.experimental.pallas import tpu as pltpu

HEAD_DIM = 128
GMLP_CHUNK = 128
N_MIXERS = 2
EPS = 1e-5
HGRN_CHUNK = 128
SUBLANES = 8

V7X_VMEM_LIMIT_BYTES = 56 * 1024 * 1024

F32 = jnp.float32
BF16 = jnp.bfloat16


def _params(sem):
    return pltpu.CompilerParams(dimension_semantics=sem, vmem_limit_bytes=V7X_VMEM_LIMIT_BYTES)


def _sigmoid(x):
    return 1.0 / (1.0 + jnp.exp(-x))


def _lower_bounds_kernel(lb_ref, o_ref):
    x = lb_ref[...]
    m = jnp.max(x, axis=0, keepdims=True)
    e = jnp.exp(x - m)
    p = e / jnp.sum(e, axis=0, keepdims=True)
    depth = x.shape[0]
    run = p[0:1, :]
    rows = [run - p[0:1, :]]
    for l in range(1, depth):
        run = run + p[l:l + 1, :]
        rows.append(run - p[0:1, :])
    o_ref[...] = jnp.concatenate(rows, axis=0)


def _lower_bounds(lb):
    return pl.pallas_call(
        _lower_bounds_kernel,
        out_shape=jax.ShapeDtypeStruct(lb.shape, F32),
        name="lower_bounds",
    )(lb.astype(F32))


def _row_stats_kernel(x_ref, xb_ref, ss_ref):
    x = x_ref[...]
    xb_ref[...] = x.astype(xb_ref.dtype)
    ss_ref[...] = jnp.sum(x * x, axis=-1, keepdims=True)


def _row_stats(x):
    s, d = x.shape
    bm = min(256, s)
    return pl.pallas_call(
        _row_stats_kernel,
        grid=(s // bm,),
        in_specs=[pl.BlockSpec((bm, d), lambda i: (i, 0))],
        out_specs=(pl.BlockSpec((bm, d), lambda i: (i, 0)), pl.BlockSpec((bm, 1), lambda i: (i, 0))),
        out_shape=(jax.ShapeDtypeStruct((s, d), BF16), jax.ShapeDtypeStruct((s, 1), F32)),
        compiler_params=_params(("parallel",)),
        name="row_stats",
    )(x)


def _rmsnorm_kernel(x_ref, w_ref, o_ref):
    x = x_ref[...]
    ms = jnp.mean(x * x, axis=-1, keepdims=True)
    o_ref[...] = (x * lax.rsqrt(ms + EPS) * w_ref[...]).astype(o_ref.dtype)


def _rmsnorm(x, w, out_dtype):
    s, d = x.shape
    bm = min(256, s)
    return pl.pallas_call(
        _rmsnorm_kernel,
        grid=(s // bm,),
        in_specs=[pl.BlockSpec((bm, d), lambda i: (i, 0)),
                  pl.BlockSpec((1, d), lambda i: (0, 0))],
        out_specs=pl.BlockSpec((bm, d), lambda i: (i, 0)),
        out_shape=jax.ShapeDtypeStruct((s, d), out_dtype),
        compiler_params=_params(("parallel",)),
        name="rmsnorm",
    )(x, w.reshape(1, d).astype(F32))


def _ep_silu(acc):
    return (acc * _sigmoid(acc),)


def _ep_identity(acc):
    return (acc,)


def _ep_forget(acc, lb):
    forget = lb + (1.0 - lb) * _sigmoid(acc)
    return (jnp.log(forget), 1.0 - forget)


def _ep_gelu(acc):
    return (jax.nn.gelu(acc),)


def _ep_relu2(acc):
    r = jnp.maximum(acc, 0.0)
    return (r * r,)


def _proj_block(a_ref, wb, ss_ref, extra, outs, epilogue, inv_k):
    acc = jnp.dot(a_ref[...], wb, preferred_element_type=F32)
    rstd = lax.rsqrt(ss_ref[...] * inv_k + EPS)
    res = epilogue(acc * rstd, *[e[...] for e in extra])
    for o_ref, r in zip(outs, res):
        o_ref[...] = r.astype(o_ref.dtype)


def _proj_head_kernel(a_ref, w_ref, nw_ref, ss_ref, *rest, epilogue, n_extra, inv_k):
    extra = rest[:n_extra]
    outs = rest[n_extra:-1]
    wb_ref = rest[-1]
    wb = (w_ref[...] * nw_ref[...]).astype(BF16)
    wb_ref[...] = wb
    _proj_block(a_ref, wb, ss_ref, extra, outs, epilogue, inv_k)


def _proj_main_kernel(a_ref, wb_ref, ss_ref, *rest, epilogue, n_extra, n_out, inv_k):
    extra = rest[:n_extra]
    head_outs = rest[n_extra:n_extra + n_out]
    outs = rest[n_extra + n_out:]
    first_rows = pl.program_id(0) == 0

    @pl.when(first_rows)
    def _():
        for o_ref, hd_ref in zip(outs, head_outs):
            o_ref[...] = hd_ref[...]

    @pl.when(jnp.logical_not(first_rows))
    def _():
        _proj_block(a_ref, wb_ref[...], ss_ref, extra, outs, epilogue, inv_k)


def _proj(hb, ss, w, layer, nw, *, n_off=0, n=None, epilogue, extra_rows=(), out_dtypes, bm=1024, bn=1024,
          bn_head=512, name):
    m, kdim = hb.shape
    n = w.shape[2] if n is None else n
    bm, bn, bnh = min(bm, m), min(bn, n), min(bn_head, n)
    n_extra, n_out = len(extra_rows), len(out_dtypes)
    off = n_off // bnh
    head = pl.pallas_call(
        functools.partial(_proj_head_kernel, epilogue=epilogue, n_extra=n_extra, inv_k=1.0 / kdim),
        grid=(n // bnh,),
        in_specs=[pl.BlockSpec((bm, kdim), lambda j: (0, 0)),
                  pl.BlockSpec((None, kdim, bnh), lambda j: (layer, 0, j + off)),
                  pl.BlockSpec((kdim, 1), lambda j: (0, 0)),
                  pl.BlockSpec((bm, 1), lambda j: (0, 0))]
                 + [pl.BlockSpec((1, bnh), lambda j: (0, j)) for _ in extra_rows],
        out_specs=tuple(pl.BlockSpec((bm, bnh), lambda j: (0, j)) for _ in out_dtypes)
                  + (pl.BlockSpec((kdim, bnh), lambda j: (0, j)),),
        out_shape=tuple(jax.ShapeDtypeStruct((bm, n), dt) for dt in out_dtypes)
                  + (jax.ShapeDtypeStruct((kdim, n), BF16),),
        compiler_params=_params(("parallel",)),
        name=name + "_head",
    )(hb, w, nw.reshape(kdim, 1).astype(F32), ss, *extra_rows)
    head_outs, wb = head[:-1], head[-1]
    if m == bm:
        return head_outs
    rest_cols = lambda i, j: jnp.where(i == 0, 0, j)
    head_cols = lambda i, j: jnp.where(i == 0, j, 0)
    return pl.pallas_call(
        functools.partial(_proj_main_kernel, epilogue=epilogue, n_extra=n_extra, n_out=n_out, inv_k=1.0 / kdim),
        grid=(m // bm, n // bn),
        in_specs=[pl.BlockSpec((bm, kdim), lambda i, j: (i, 0)),
                  pl.BlockSpec((kdim, bn), lambda i, j: (0, rest_cols(i, j))),
                  pl.BlockSpec((bm, 1), lambda i, j: (i, 0))]
                 + [pl.BlockSpec((1, bn), lambda i, j: (0, j)) for _ in extra_rows]
                 + [pl.BlockSpec((bm, bn), lambda i, j: (0, head_cols(i, j)), pipeline_mode=pl.Buffered(1))
                    for _ in out_dtypes],
        out_specs=tuple(pl.BlockSpec((bm, bn), lambda i, j: (i, j)) for _ in out_dtypes),
        out_shape=tuple(jax.ShapeDtypeStruct((m, n), dt) for dt in out_dtypes),
        compiler_params=_params(("parallel", "parallel")),
        name=name,
    )(hb, wb, ss, *extra_rows, *head_outs)


def _emit_residual(acc, res_ref, h_ref, hb_ref, ss_ref, first_col_block):
    h = res_ref[...] + acc
    h_ref[...] = h
    hb_ref[...] = h.astype(hb_ref.dtype)
    part = jnp.sum(h * h, axis=-1, keepdims=True)

    @pl.when(first_col_block)
    def _():
        ss_ref[...] = part

    @pl.when(jnp.logical_not(first_col_block))
    def _():
        ss_ref[...] += part


def _accumulate(acc_ref, partial_product, k, nk, emit):
    if nk == 1:
        emit(partial_product())
        return

    @pl.when(k == 0)
    def _():
        acc_ref[...] = partial_product()

    @pl.when(jnp.logical_and(k > 0, k < nk - 1))
    def _():
        acc_ref[...] += partial_product()

    @pl.when(k == nk - 1)
    def _():
        emit(acc_ref[...] + partial_product())


def _out_head_kernel(a_ref, w_ref, res_ref, h_ref, hb_ref, ss_ref, wb_ref, acc_ref, *, nk):
    def partial_product():
        wb = w_ref[...].astype(BF16)
        wb_ref[...] = wb
        return jnp.dot(a_ref[...], wb, preferred_element_type=F32)

    first = pl.program_id(0) == 0
    _accumulate(acc_ref, partial_product, pl.program_id(1), nk,
                lambda acc: _emit_residual(acc, res_ref, h_ref, hb_ref, ss_ref, first))


def _out_main_kernel(a_ref, wb_ref, res_ref, h_head, hb_head, ss_head, h_ref, hb_ref, ss_ref, acc_ref, *, nk):
    first_rows = pl.program_id(0) == 0
    first_cols = pl.program_id(1) == 0
    last_k = pl.program_id(2) == nk - 1

    @pl.when(jnp.logical_and(first_rows, last_k))
    def _():
        h_ref[...] = h_head[...]
        hb_ref[...] = hb_head[...]

        @pl.when(first_cols)
        def _():
            ss_ref[...] = ss_head[...]

    @pl.when(jnp.logical_not(first_rows))
    def _():
        _accumulate(acc_ref, lambda: jnp.dot(a_ref[...], wb_ref[...], preferred_element_type=F32),
                    pl.program_id(2), nk,
                    lambda acc: _emit_residual(acc, res_ref, h_ref, hb_ref, ss_ref, first_cols))


def _out_proj(a, w, layer, res, *, bm=1024, bn=1024, bk=2048, bn_head=512, bk_head=2048, name):
    m, kdim = a.shape
    n = w.shape[2]
    bm, bn, bnh, bk, bkh = min(bm, m), min(bn, n), min(bn_head, n), min(bk, kdim), min(bk_head, kdim)
    nk, nkh = kdim // bk, kdim // bkh
    h, hb, ss, wb = pl.pallas_call(
        functools.partial(_out_head_kernel, nk=nkh),
        grid=(n // bnh, nkh),
        in_specs=[pl.BlockSpec((bm, bkh), lambda j, k: (0, k)),
                  pl.BlockSpec((None, bkh, bnh), lambda j, k: (layer, k, j)),
                  pl.BlockSpec((bm, bnh), lambda j, k: (0, j))],
        out_specs=(pl.BlockSpec((bm, bnh), lambda j, k: (0, j)),
                   pl.BlockSpec((bm, bnh), lambda j, k: (0, j)),
                   pl.BlockSpec((bm, 1), lambda j, k: (0, 0)),
                   pl.BlockSpec((bkh, bnh), lambda j, k: (k, j))),
        out_shape=(jax.ShapeDtypeStruct((bm, n), F32), jax.ShapeDtypeStruct((bm, n), BF16),
                   jax.ShapeDtypeStruct((bm, 1), F32), jax.ShapeDtypeStruct((kdim, n), BF16)),
        scratch_shapes=[pltpu.VMEM((bm, bnh), F32)],
        compiler_params=_params(("arbitrary", "arbitrary")),
        name=name + "_head",
    )(a, w, res)
    if m == bm:
        return h, hb, ss
    pin = lambda i, x: jnp.where(i == 0, 0, x)
    head_cols = lambda i, j: jnp.where(i == 0, j, 0)
    blk = lambda i, j, k: (i, j)
    return pl.pallas_call(
        functools.partial(_out_main_kernel, nk=nk),
        grid=(m // bm, n // bn, nk),
        in_specs=[pl.BlockSpec((bm, bk), lambda i, j, k: (i, pin(i, k))),
                  pl.BlockSpec((bk, bn), lambda i, j, k: (pin(i, k), pin(i, j))),
                  pl.BlockSpec((bm, bn), lambda i, j, k: (i, pin(i, j))),
                  pl.BlockSpec((bm, bn), lambda i, j, k: (0, head_cols(i, j)), pipeline_mode=pl.Buffered(1)),
                  pl.BlockSpec((bm, bn), lambda i, j, k: (0, head_cols(i, j)), pipeline_mode=pl.Buffered(1)),
                  pl.BlockSpec((bm, 1), lambda i, j, k: (0, 0))],
        out_specs=(pl.BlockSpec((bm, bn), blk), pl.BlockSpec((bm, bn), blk),
                   pl.BlockSpec((bm, 1), lambda i, j, k: (i, 0))),
        out_shape=(jax.ShapeDtypeStruct((m, n), F32), jax.ShapeDtypeStruct((m, n), BF16),
                   jax.ShapeDtypeStruct((m, 1), F32)),
        scratch_shapes=[pltpu.VMEM((bm, bn), F32)],
        compiler_params=_params(("parallel", "arbitrary", "arbitrary")),
        name=name,
    )(a, wb, res, h, hb, ss)


MXU_LEVEL_HALF_SIZES = (4, 2)


def _hgrn_constants(c):
    n_levels = int(math.log2(c))
    t = np.arange(c)[:, None]
    j = np.arange(c)[None, :]
    blocks = [(j <= t)]
    level = np.full((c, c), -1, np.int32)
    for l in range(n_levels):
        m = c >> (l + 1)
        r = (t // (2 * m)) * (2 * m) + m - 1
        second = (t % (2 * m)) >= m
        if m in MXU_LEVEL_HALF_SIZES:
            blocks.append(np.where(second, (j > r) & (j <= t), (j > t) & (j <= r)))
        same = (t // (2 * m)) == (j // (2 * m))
        level[same & second & ((j % (2 * m)) < m)] = l
    level[np.arange(c), np.arange(c)] = n_levels
    cst = np.concatenate(blocks, axis=0).astype(np.float32)
    return np.concatenate([cst, cst], axis=1), level, n_levels


def _dot_nt(a, b):
    return lax.dot_general(a, b, (((1,), (1,)), ((), ())), preferred_element_type=F32)


def _dot_tn(a, b):
    return lax.dot_general(a, b, (((0,), (0,)), ((), ())), preferred_element_type=F32)


def _level_exponents(p, p_ref, lf, ex, c, n_levels):
    out = []
    for l in range(n_levels):
        m = c >> (l + 1)
        if m >= SUBLANES:
            pieces = []
            for b in range(c // (2 * m)):
                lo_row = 2 * m * b
                pr = p_ref[lo_row + m - 1:lo_row + m, :]
                pieces.append(pr - p[lo_row:lo_row + m])
                pieces.append(p[lo_row + m:lo_row + 2 * m] - pr)
            out.append(jnp.concatenate(pieces, axis=0))
        elif m in MXU_LEVEL_HALF_SIZES:
            i = 1 + MXU_LEVEL_HALF_SIZES.index(m)
            out.append(ex[i * c:(i + 1) * c])
        else:
            odd = (lax.broadcasted_iota(jnp.int32, lf.shape, 0) & 1) == 1
            out.append(jnp.where(odd, lf, 0.0))
    return out


def _hgrn_kernel(q_ref, k_ref, v_ref, g_ref, lf_ref, cm_ref, lv_ref, gn_ref, o_ref, st_ref, p_ref, *, c, n_levels):
    @pl.when(pl.program_id(1) == 0)
    def _():
        st_ref[...] = jnp.zeros_like(st_ref)

    t_blk, w = q_ref.shape
    n_heads = w // HEAD_DIM
    lv = lv_ref[...]
    gn = gn_ref[...]

    def chunk(ci, carry):
        r0 = pl.multiple_of(ci * c, c)
        rows = pl.ds(r0, c)
        lf = lf_ref[rows, :]
        hi = lf.astype(BF16)
        lo = (lf - hi.astype(F32)).astype(BF16)
        ex = jnp.dot(cm_ref[...], jnp.concatenate([hi, lo], axis=0), preferred_element_type=F32)
        p = ex[0:c]
        p_ref[...] = p
        dec_q = jnp.exp(p).astype(BF16)
        p_last = p_ref[c - 1:c, :]
        dec_k = jnp.exp(p_last - p).astype(BF16)
        dec_end = jnp.exp(p_last)
        dec_l = [jnp.exp(e).astype(BF16) for e in _level_exponents(p, p_ref, lf, ex, c, n_levels)]
        for h in range(n_heads):
            sl = slice(h * HEAD_DIM, (h + 1) * HEAD_DIM)
            q = q_ref[rows, sl]
            k = k_ref[rows, sl]
            v = v_ref[rows, sl]
            st = st_ref[h]
            o = _dot_nt(q * dec_q[:, sl], st.astype(BF16))
            kt = k.T
            s = jnp.where(lv == n_levels, jnp.dot(q, kt, preferred_element_type=F32), 0.0)
            for l in range(n_levels):
                d = dec_l[l][:, sl]
                s = jnp.where(lv == l, jnp.dot(q * d, kt * d.T, preferred_element_type=F32), s)
            o = o + jnp.dot(s.astype(BF16), v, preferred_element_type=F32)
            st_ref[h] = st * dec_end[:, sl] + _dot_tn(v, k * dec_k[:, sl])
            ms = jnp.mean(o * o, axis=-1, keepdims=True)
            o = o * lax.rsqrt(ms + EPS) * gn * g_ref[rows, sl].astype(F32)
            o_ref[rows, sl] = o.astype(o_ref.dtype)
        return carry

    lax.fori_loop(0, t_blk // c, chunk, 0, unroll=True)


def _hgrn_recurrence(q, k, v, g, lf, gnorm, *, c=HGRN_CHUNK, t_blk=512, w=512):
    s, d = q.shape
    c = min(c, s)
    assert c >= 2 * SUBLANES and c & (c - 1) == 0, "chunk must be a power of two of at least two sublane tiles"
    t_blk = min(t_blk, s)
    w = min(w, d)
    cm, lv, n_levels = _hgrn_constants(c)
    blk = pl.BlockSpec((t_blk, w), lambda hi, ti: (ti, hi))
    return pl.pallas_call(
        functools.partial(_hgrn_kernel, c=c, n_levels=n_levels),
        grid=(d // w, s // t_blk),
        in_specs=[blk, blk, blk, blk, blk,
                  pl.BlockSpec(cm.shape, lambda hi, ti: (0, 0)),
                  pl.BlockSpec(lv.shape, lambda hi, ti: (0, 0)),
                  pl.BlockSpec((1, HEAD_DIM), lambda hi, ti: (0, 0))],
        out_specs=blk,
        out_shape=jax.ShapeDtypeStruct((s, d), BF16),
        scratch_shapes=[pltpu.VMEM((w // HEAD_DIM, HEAD_DIM, HEAD_DIM), F32), pltpu.VMEM((c, w), F32)],
        compiler_params=_params(("parallel", "arbitrary")),
        name="hgrn_recurrence",
    )(q, k, v, g, lf, jnp.asarray(cm, BF16), jnp.asarray(lv), gnorm.reshape(1, HEAD_DIM).astype(F32))


def _gmlp_kernel(z_ref, lng_ref, lnb_ref, wsp_ref, bsp_ref, o_ref, wc_ref):
    @pl.when(pl.program_id(0) == 0)
    def _():
        n = wsp_ref.shape[1]
        tri = lax.broadcasted_iota(jnp.int32, (n, n), 0) >= lax.broadcasted_iota(jnp.int32, (n, n), 1)
        for g in range(wsp_ref.shape[0]):
            wc_ref[g] = jnp.where(tri, wsp_ref[g], 0.0).astype(BF16)

    t_blk, d2 = z_ref.shape
    d = d2 // 2
    n_groups = d // HEAD_DIM
    for ci in range(t_blk // GMLP_CHUNK):
        rows = slice(ci * GMLP_CHUNK, (ci + 1) * GMLP_CHUNK)
        v = z_ref[rows, d:].astype(F32)
        mu = jnp.mean(v, axis=-1, keepdims=True)
        vc = v - mu
        var = jnp.mean(vc * vc, axis=-1, keepdims=True)
        rstd = lax.rsqrt(var + EPS)
        for g in range(n_groups):
            sl = slice(g * HEAD_DIM, (g + 1) * HEAD_DIM)
            vn = vc[:, sl] * rstd * lng_ref[:, sl] + lnb_ref[:, sl]
            mixed = jnp.dot(wc_ref[g], vn.astype(BF16), preferred_element_type=F32) + bsp_ref[g]
            o_ref[rows, sl] = (z_ref[rows, sl].astype(F32) * mixed).astype(o_ref.dtype)


def _gmlp_mix(z, ln_g, ln_b, w_sp, b_sp, *, t_blk=256):
    s, d2 = z.shape
    d = d2 // 2
    g, n, _ = w_sp.shape
    t_blk = min(t_blk, s)
    bsp = jnp.broadcast_to(b_sp.astype(F32)[:, :, None], (g, n, HEAD_DIM))
    return pl.pallas_call(
        _gmlp_kernel,
        grid=(s // t_blk,),
        in_specs=[pl.BlockSpec((t_blk, d2), lambda i: (i, 0)),
                  pl.BlockSpec((1, d), lambda i: (0, 0)),
                  pl.BlockSpec((1, d), lambda i: (0, 0)),
                  pl.BlockSpec((g, n, n), lambda i: (0, 0, 0)),
                  pl.BlockSpec((g, n, HEAD_DIM), lambda i: (0, 0, 0))],
        out_specs=pl.BlockSpec((t_blk, d), lambda i: (i, 0)),
        out_shape=jax.ShapeDtypeStruct((s, d), BF16),
        scratch_shapes=[pltpu.VMEM((g, n, n), BF16)],
        compiler_params=_params(("arbitrary",)),
        name="gmlp_mix",
    )(z, ln_g.reshape(1, d).astype(F32), ln_b.reshape(1, d).astype(F32), w_sp.astype(F32), bsp)


def kernel(x, norm_mix, norm_ffn, hgrn_w_in, hgrn_lower_bounds, hgrn_gnorm, hgrn_w_out, gmlp_w_in,
           gmlp_ln_gain, gmlp_ln_bias, gmlp_w_spatial, gmlp_b_spatial, gmlp_w_out, ffn_w1, ffn_w2, norm_final):
    bsz, s, d = x.shape
    assert bsz == 1, "the recurrence kernel carries one sequence"
    depth = norm_mix.shape[0]
    h = x.reshape(s, d).astype(F32)
    lower_bounds = _lower_bounds(hgrn_lower_bounds)
    hb, ss = _row_stats(h)

    for layer in range(depth):
        j = layer // N_MIXERS
        nw = norm_mix[layer]
        if layer % N_MIXERS == 0:
            lb = lower_bounds[layer].reshape(1, d)
            (q,) = _proj(hb, ss, hgrn_w_in, j, nw, n_off=0, n=d, epilogue=_ep_silu, out_dtypes=(BF16,),
                         name="hgrn_q")
            lf, k = _proj(hb, ss, hgrn_w_in, j, nw, n_off=d, n=d, epilogue=_ep_forget, extra_rows=(lb,),
                          out_dtypes=(F32, BF16), bn=512, name="hgrn_f")
            (v,) = _proj(hb, ss, hgrn_w_in, j, nw, n_off=2 * d, n=d, epilogue=_ep_identity, out_dtypes=(BF16,),
                         name="hgrn_i")
            (g,) = _proj(hb, ss, hgrn_w_in, j, nw, n_off=3 * d, n=d, epilogue=_ep_silu, out_dtypes=(BF16,),
                         name="hgrn_g")
            mixed = _hgrn_recurrence(q, k, v, g, lf, hgrn_gnorm[j])
            w_out = hgrn_w_out
        else:
            (z,) = _proj(hb, ss, gmlp_w_in, j, nw, epilogue=_ep_gelu, out_dtypes=(BF16,), name="gmlp_in")
            mixed = _gmlp_mix(z, gmlp_ln_gain[j], gmlp_ln_bias[j], gmlp_w_spatial[j], gmlp_b_spatial[j])
            w_out = gmlp_w_out
        h, hb, ss = _out_proj(mixed, w_out, j, h, bn=512, bk=d, name="mix_out")
        (hid,) = _proj(hb, ss, ffn_w1, layer, norm_ffn[layer], epilogue=_ep_relu2, out_dtypes=(BF16,),
                       name="ffn_up")
        h, hb, ss = _out_proj(hid, ffn_w2, layer, h, name="ffn_down")
    out = _rmsnorm(h, norm_final, F32)
    return out.reshape(bsz, s, d)
```

```python
import functools
import math

import numpy as np
import jax
import jax.numpy as jnp
from jax import lax
from jax.experimental import pallas as pl
from jax.experimental.pallas import tpu as pltpu

HEAD_DIM = 128
GMLP_CHUNK = 128
N_MIXERS = 2
EPS = 1e-5
HGRN_CHUNK = 128
SUBLANES = 8
BF16_ROW_TILE = 16

V7X_VMEM_LIMIT_BYTES = 56 * 1024 * 1024

F32 = jnp.float32
BF16 = jnp.bfloat16


def _params(sem):
    return pltpu.CompilerParams(dimension_semantics=sem, vmem_limit_bytes=V7X_VMEM_LIMIT_BYTES)


def _sigmoid(x):
    return 1.0 / (1.0 + jnp.exp(-x))


def _lower_bounds_kernel(lb_ref, o_ref):
    x = lb_ref[...]
    m = jnp.max(x, axis=0, keepdims=True)
    e = jnp.exp(x - m)
    p = e / jnp.sum(e, axis=0, keepdims=True)
    depth = x.shape[0]
    run = p[0:1, :]
    rows = [run - p[0:1, :]]
    for l in range(1, depth):
        run = run + p[l:l + 1, :]
        rows.append(run - p[0:1, :])
    o_ref[...] = jnp.concatenate(rows, axis=0)


def _lower_bounds(lb):
    return pl.pallas_call(
        _lower_bounds_kernel,
        out_shape=jax.ShapeDtypeStruct(lb.shape, F32),
        name="lower_bounds",
    )(lb.astype(F32))


def _row_stats_kernel(x_ref, xb_ref, ss_ref):
    x = x_ref[...]
    xb_ref[...] = x.astype(xb_ref.dtype)
    ss_ref[...] = jnp.sum(x * x, axis=-1, keepdims=True)


def _row_stats(x):
    s, d = x.shape
    bm = min(256, s)
    return pl.pallas_call(
        _row_stats_kernel,
        grid=(s // bm,),
        in_specs=[pl.BlockSpec((bm, d), lambda i: (i, 0))],
        out_specs=(pl.BlockSpec((bm, d), lambda i: (i, 0)), pl.BlockSpec((bm, 1), lambda i: (i, 0))),
        out_shape=(jax.ShapeDtypeStruct((s, d), BF16), jax.ShapeDtypeStruct((s, 1), F32)),
        compiler_params=_params(("parallel",)),
        name="row_stats",
    )(x)


def _rmsnorm_kernel(x_ref, w_ref, o_ref):
    x = x_ref[...]
    ms = jnp.mean(x * x, axis=-1, keepdims=True)
    o_ref[...] = (x * lax.rsqrt(ms + EPS) * w_ref[...]).astype(o_ref.dtype)


def _rmsnorm(x, w, out_dtype):
    s, d = x.shape
    bm = min(256, s)
    return pl.pallas_call(
        _rmsnorm_kernel,
        grid=(s // bm,),
        in_specs=[pl.BlockSpec((bm, d), lambda i: (i, 0)),
                  pl.BlockSpec((1, d), lambda i: (0, 0))],
        out_specs=pl.BlockSpec((bm, d), lambda i: (i, 0)),
        out_shape=jax.ShapeDtypeStruct((s, d), out_dtype),
        compiler_params=_params(("parallel",)),
        name="rmsnorm",
    )(x, w.reshape(1, d).astype(F32))


def _ep_silu(acc):
    return (acc * _sigmoid(acc),)


def _ep_identity(acc):
    return (acc,)


def _ep_forget(acc, lb):
    forget = lb + (1.0 - lb) * _sigmoid(acc)
    return (jnp.log(forget), 1.0 - forget)


def _ep_gelu(acc):
    return (jax.nn.gelu(acc),)


def _ep_relu2(acc):
    r = jnp.maximum(acc, 0.0)
    return (r * r,)


def _convert_slab(w_ref, nw_ref, o_ref):
    w = w_ref[...]
    if nw_ref is not None:
        w = w * nw_ref[...]
    o_ref[...] = w.astype(o_ref.dtype)


def _cast_kernel(w_ref, nw_ref, o_ref):
    _convert_slab(w_ref, nw_ref, o_ref)


def _cast_fold(w, layer, nw):
    _, r, c = w.shape
    br = min(r, max(BF16_ROW_TILE, (2 * 1024 * 1024) // c))
    return pl.pallas_call(
        _cast_kernel,
        grid=(r // br,),
        in_specs=[pl.BlockSpec((None, br, c), lambda i: (layer, i, 0)),
                  pl.BlockSpec((br, 1), lambda i: (i, 0))],
        out_specs=pl.BlockSpec((br, c), lambda i: (i, 0)),
        out_shape=jax.ShapeDtypeStruct((r, c), BF16),
        compiler_params=_params(("parallel",)),
        name="cast_fold",
    )(w, nw.reshape(r, 1).astype(F32))


class _Side:
    def __init__(self, w, layer, nw=None):
        self.w, self.layer, self.nw = w, layer, nw

    def plan(self, n_blocks, block_of):
        _, r, c = self.w.shape
        assert r % n_blocks == 0 and (r // n_blocks) % BF16_ROW_TILE == 0, (r, n_blocks)
        br = r // n_blocks
        layer = self.layer
        in_specs = [pl.BlockSpec((None, br, c), lambda *idx: (layer, block_of(*idx), 0))]
        operands = [self.w]
        if self.nw is not None:
            in_specs.append(pl.BlockSpec((br, 1), lambda *idx: (block_of(*idx), 0)))
            operands.append(self.nw.reshape(r, 1).astype(F32))
        out_spec = pl.BlockSpec((br, c), lambda *idx: (block_of(*idx), 0))
        return in_specs, out_spec, jax.ShapeDtypeStruct((r, c), BF16), operands


def _proj_kernel(a_ref, wb_ref, ss_ref, *rest, epilogue, n_extra, n_out, inv_k, side_nw):
    extra = rest[:n_extra]
    n_side_in = 0 if side_nw is None else (2 if side_nw else 1)
    side_in = rest[n_extra:n_extra + n_side_in]
    outs = rest[n_extra + n_side_in:n_extra + n_side_in + n_out]
    if n_side_in:
        _convert_slab(side_in[0], side_in[1] if side_nw else None, rest[-1])
    acc = jnp.dot(a_ref[...], wb_ref[...], preferred_element_type=F32)
    rstd = lax.rsqrt(ss_ref[...] * inv_k + EPS)
    res = epilogue(acc * rstd, *[e[...] for e in extra])
    for o_ref, r in zip(outs, res):
        o_ref[...] = r.astype(o_ref.dtype)


def _proj(hb, ss, wb, *, n_off=0, n=None, epilogue, extra_rows=(), out_dtypes, bm=1024, bn=1024, side=None, name):
    m, kdim = hb.shape
    n = wb.shape[1] if n is None else n
    bm, bn = min(bm, m), min(bn, n)
    ni, nj = m // bm, n // bn
    off = n_off // bn
    in_specs = [pl.BlockSpec((bm, kdim), lambda i, j: (i, 0)),
                pl.BlockSpec((kdim, bn), lambda i, j: (0, j + off)),
                pl.BlockSpec((bm, 1), lambda i, j: (i, 0))]
    in_specs += [pl.BlockSpec((1, bn), lambda i, j: (0, j)) for _ in extra_rows]
    operands = [hb, wb, ss, *extra_rows]
    out_specs = [pl.BlockSpec((bm, bn), lambda i, j: (i, j)) for _ in out_dtypes]
    out_shape = [jax.ShapeDtypeStruct((m, n), dt) for dt in out_dtypes]
    side_nw = None
    if side is not None:
        s_in, s_out, s_shape, s_ops = side.plan(ni * nj, lambda i, j: i * nj + j)
        in_specs += s_in
        operands += s_ops
        out_specs.append(s_out)
        out_shape.append(s_shape)
        side_nw = side.nw is not None
    return pl.pallas_call(
        functools.partial(_proj_kernel, epilogue=epilogue, n_extra=len(extra_rows), n_out=len(out_dtypes),
                          inv_k=1.0 / kdim, side_nw=side_nw),
        grid=(ni, nj),
        in_specs=in_specs,
        out_specs=tuple(out_specs),
        out_shape=tuple(out_shape),
        compiler_params=_params(("parallel", "parallel")),
        name=name,
    )(*operands)


def _emit_residual(acc, res_ref, h_ref, hb_ref, ss_ref, first_col_block):
    h = res_ref[...] + acc
    h_ref[...] = h
    hb_ref[...] = h.astype(hb_ref.dtype)
    part = jnp.sum(h * h, axis=-1, keepdims=True)

    @pl.when(first_col_block)
    def _():
        ss_ref[...] = part

    @pl.when(jnp.logical_not(first_col_block))
    def _():
        ss_ref[...] += part


def _accumulate(acc_ref, partial_product, k, nk, emit):
    if nk == 1:
        emit(partial_product())
        return

    @pl.when(k == 0)
    def _():
        acc_ref[...] = partial_product()

    @pl.when(jnp.logical_and(k > 0, k < nk - 1))
    def _():
        acc_ref[...] += partial_product()

    @pl.when(k == nk - 1)
    def _():
        emit(acc_ref[...] + partial_product())


def _out_kernel(a_ref, wb_ref, res_ref, *rest, nk, side_nw, side_every):
    n_side_in = 0 if side_nw is None else (2 if side_nw else 1)
    side_in = rest[:n_side_in]
    h_ref, hb_ref, ss_ref = rest[n_side_in:n_side_in + 3]
    acc_ref = rest[-1]
    k = pl.program_id(2)
    if n_side_in:
        @pl.when(k % side_every == 0)
        def _():
            _convert_slab(side_in[0], side_in[1] if side_nw else None, rest[-2])

    first_cols = pl.program_id(1) == 0
    _accumulate(acc_ref, lambda: jnp.dot(a_ref[...], wb_ref[...], preferred_element_type=F32), k, nk,
                lambda acc: _emit_residual(acc, res_ref, h_ref, hb_ref, ss_ref, first_cols))


def _out_proj(a, wb, res, *, bm=1024, bn=1024, bk=2048, side=None, side_splits=1, name):
    m, kdim = a.shape
    n = wb.shape[1]
    bm, bn, bk = min(bm, m), min(bn, n), min(bk, kdim)
    ni, nj, nk = m // bm, n // bn, kdim // bk
    side_splits = math.gcd(side_splits, nk)
    side_every = nk // side_splits
    blk = lambda i, j, k: (i, j)
    in_specs = [pl.BlockSpec((bm, bk), lambda i, j, k: (i, k)),
                pl.BlockSpec((bk, bn), lambda i, j, k: (k, j)),
                pl.BlockSpec((bm, bn), blk)]
    operands = [a, wb, res]
    out_specs = [pl.BlockSpec((bm, bn), blk), pl.BlockSpec((bm, bn), blk),
                 pl.BlockSpec((bm, 1), lambda i, j, k: (i, 0))]
    out_shape = [jax.ShapeDtypeStruct((m, n), F32), jax.ShapeDtypeStruct((m, n), BF16),
                 jax.ShapeDtypeStruct((m, 1), F32)]
    side_nw = None
    if side is not None:
        s_in, s_out, s_shape, s_ops = side.plan(
            ni * nj * side_splits, lambda i, j, k: (i * nj + j) * side_splits + k // side_every)
        in_specs += s_in
        operands += s_ops
        out_specs.append(s_out)
        out_shape.append(s_shape)
        side_nw = side.nw is not None
    return pl.pallas_call(
        functools.partial(_out_kernel, nk=nk, side_nw=side_nw, side_every=side_every),
        grid=(ni, nj, nk),
        in_specs=in_specs,
        out_specs=tuple(out_specs),
        out_shape=tuple(out_shape),
        scratch_shapes=[pltpu.VMEM((bm, bn), F32)],
        compiler_params=_params(("parallel", "arbitrary", "arbitrary")),
        name=name,
    )(*operands)


MXU_LEVEL_HALF_SIZES = (4, 2)


def _hgrn_constants(c):
    n_levels = int(math.log2(c))
    t = np.arange(c)[:, None]
    j = np.arange(c)[None, :]
    blocks = [(j <= t)]
    level = np.full((c, c), -1, np.int32)
    for l in range(n_levels):
        m = c >> (l + 1)
        r = (t // (2 * m)) * (2 * m) + m - 1
        second = (t % (2 * m)) >= m
        if m in MXU_LEVEL_HALF_SIZES:
            blocks.append(np.where(second, (j > r) & (j <= t), (j > t) & (j <= r)))
        same = (t // (2 * m)) == (j // (2 * m))
        level[same & second & ((j % (2 * m)) < m)] = l
    level[np.arange(c), np.arange(c)] = n_levels
    cst = np.concatenate(blocks, axis=0).astype(np.float32)
    return np.concatenate([cst, cst], axis=1), level, n_levels


def _dot_nt(a, b):
    return lax.dot_general(a, b, (((1,), (1,)), ((), ())), preferred_element_type=F32)


def _dot_tn(a, b):
    return lax.dot_general(a, b, (((0,), (0,)), ((), ())), preferred_element_type=F32)


def _level_exponents(p, p_ref, lf, ex, c, n_levels):
    out = []
    for l in range(n_levels):
        m = c >> (l + 1)
        if m >= SUBLANES:
            pieces = []
            for b in range(c // (2 * m)):
                lo_row = 2 * m * b
                pr = p_ref[lo_row + m - 1:lo_row + m, :]
                pieces.append(pr - p[lo_row:lo_row + m])
                pieces.append(p[lo_row + m:lo_row + 2 * m] - pr)
            out.append(jnp.concatenate(pieces, axis=0))
        elif m in MXU_LEVEL_HALF_SIZES:
            i = 1 + MXU_LEVEL_HALF_SIZES.index(m)
            out.append(ex[i * c:(i + 1) * c])
        else:
            odd = (lax.broadcasted_iota(jnp.int32, lf.shape, 0) & 1) == 1
            out.append(jnp.where(odd, lf, 0.0))
    return out


def _hgrn_kernel(q_ref, k_ref, v_ref, g_ref, lf_ref, cm_ref, lv_ref, gn_ref, o_ref, st_ref, p_ref, *, c, n_levels):
    @pl.when(pl.program_id(1) == 0)
    def _():
        st_ref[...] = jnp.zeros_like(st_ref)

    t_blk, w = q_ref.shape
    n_heads = w // HEAD_DIM
    lv = lv_ref[...]
    gn = gn_ref[...]

    def chunk(ci, carry):
        r0 = pl.multiple_of(ci * c, c)
        rows = pl.ds(r0, c)
        lf = lf_ref[rows, :]
        hi = lf.astype(BF16)
        lo = (lf - hi.astype(F32)).astype(BF16)
        ex = jnp.dot(cm_ref[...], jnp.concatenate([hi, lo], axis=0), preferred_element_type=F32)
        p = ex[0:c]
        p_ref[...] = p
        dec_q = jnp.exp(p).astype(BF16)
        p_last = p_ref[c - 1:c, :]
        dec_k = jnp.exp(p_last - p).astype(BF16)
        dec_end = jnp.exp(p_last)
        dec_l = [jnp.exp(e).astype(BF16) for e in _level_exponents(p, p_ref, lf, ex, c, n_levels)]
        for h in range(n_heads):
            sl = slice(h * HEAD_DIM, (h + 1) * HEAD_DIM)
            q = q_ref[rows, sl]
            k = k_ref[rows, sl]
            v = v_ref[rows, sl]
            st = st_ref[h]
            o = _dot_nt(q * dec_q[:, sl], st.astype(BF16))
            kt = k.T
            s = jnp.where(lv == n_levels, jnp.dot(q, kt, preferred_element_type=F32), 0.0)
            for l in range(n_levels):
                d = dec_l[l][:, sl]
                s = jnp.where(lv == l, jnp.dot(q * d, kt * d.T, preferred_element_type=F32), s)
            o = o + jnp.dot(s.astype(BF16), v, preferred_element_type=F32)
            st_ref[h] = st * dec_end[:, sl] + _dot_tn(v, k * dec_k[:, sl])
            ms = jnp.mean(o * o, axis=-1, keepdims=True)
            o = o * lax.rsqrt(ms + EPS) * gn * g_ref[rows, sl].astype(F32)
            o_ref[rows, sl] = o.astype(o_ref.dtype)
        return carry

    lax.fori_loop(0, t_blk // c, chunk, 0, unroll=True)


def _hgrn_recurrence(q, k, v, g, lf, gnorm, *, c=HGRN_CHUNK, t_blk=512, w=512):
    s, d = q.shape
    c = min(c, s)
    assert c >= 2 * SUBLANES and c & (c - 1) == 0, "chunk must be a power of two of at least two sublane tiles"
    t_blk = min(t_blk, s)
    w = min(w, d)
    cm, lv, n_levels = _hgrn_constants(c)
    blk = pl.BlockSpec((t_blk, w), lambda hi, ti: (ti, hi))
    return pl.pallas_call(
        functools.partial(_hgrn_kernel, c=c, n_levels=n_levels),
        grid=(d // w, s // t_blk),
        in_specs=[blk, blk, blk, blk, blk,
                  pl.BlockSpec(cm.shape, lambda hi, ti: (0, 0)),
                  pl.BlockSpec(lv.shape, lambda hi, ti: (0, 0)),
                  pl.BlockSpec((1, HEAD_DIM), lambda hi, ti: (0, 0))],
        out_specs=blk,
        out_shape=jax.ShapeDtypeStruct((s, d), BF16),
        scratch_shapes=[pltpu.VMEM((w // HEAD_DIM, HEAD_DIM, HEAD_DIM), F32), pltpu.VMEM((c, w), F32)],
        compiler_params=_params(("parallel", "arbitrary")),
        name="hgrn_recurrence",
    )(q, k, v, g, lf, jnp.asarray(cm, BF16), jnp.asarray(lv), gnorm.reshape(1, HEAD_DIM).astype(F32))


def _gmlp_kernel(z_ref, lng_ref, lnb_ref, wsp_ref, bsp_ref, o_ref, wc_ref):
    @pl.when(pl.program_id(0) == 0)
    def _():
        n = wsp_ref.shape[1]
        tri = lax.broadcasted_iota(jnp.int32, (n, n), 0) >= lax.broadcasted_iota(jnp.int32, (n, n), 1)
        for g in range(wsp_ref.shape[0]):
            wc_ref[g] = jnp.where(tri, wsp_ref[g], 0.0).astype(BF16)

    t_blk, d2 = z_ref.shape
    d = d2 // 2
    n_groups = d // HEAD_DIM
    for ci in range(t_blk // GMLP_CHUNK):
        rows = slice(ci * GMLP_CHUNK, (ci + 1) * GMLP_CHUNK)
        v = z_ref[rows, d:].astype(F32)
        mu = jnp.mean(v, axis=-1, keepdims=True)
        vc = v - mu
        var = jnp.mean(vc * vc, axis=-1, keepdims=True)
        rstd = lax.rsqrt(var + EPS)
        for g in range(n_groups):
            sl = slice(g * HEAD_DIM, (g + 1) * HEAD_DIM)
            vn = vc[:, sl] * rstd * lng_ref[:, sl] + lnb_ref[:, sl]
            mixed = jnp.dot(wc_ref[g], vn.astype(BF16), preferred_element_type=F32) + bsp_ref[g]
            o_ref[rows, sl] = (z_ref[rows, sl].astype(F32) * mixed).astype(o_ref.dtype)


def _gmlp_mix(z, ln_g, ln_b, w_sp, b_sp, *, t_blk=256):
    s, d2 = z.shape
    d = d2 // 2
    g, n, _ = w_sp.shape
    t_blk = min(t_blk, s)
    bsp = jnp.broadcast_to(b_sp.astype(F32)[:, :, None], (g, n, HEAD_DIM))
    return pl.pallas_call(
        _gmlp_kernel,
        grid=(s // t_blk,),
        in_specs=[pl.BlockSpec((t_blk, d2), lambda i: (i, 0)),
                  pl.BlockSpec((1, d), lambda i: (0, 0)),
                  pl.BlockSpec((1, d), lambda i: (0, 0)),
                  pl.BlockSpec((g, n, n), lambda i: (0, 0, 0)),
                  pl.BlockSpec((g, n, HEAD_DIM), lambda i: (0, 0, 0))],
        out_specs=pl.BlockSpec((t_blk, d), lambda i: (i, 0)),
        out_shape=jax.ShapeDtypeStruct((s, d), BF16),
        scratch_shapes=[pltpu.VMEM((g, n, n), BF16)],
        compiler_params=_params(("arbitrary",)),
        name="gmlp_mix",
    )(z, ln_g.reshape(1, d).astype(F32), ln_b.reshape(1, d).astype(F32), w_sp.astype(F32), bsp)


def kernel(x, norm_mix, norm_ffn, hgrn_w_in, hgrn_lower_bounds, hgrn_gnorm, hgrn_w_out, gmlp_w_in,
           gmlp_ln_gain, gmlp_ln_bias, gmlp_w_spatial, gmlp_b_spatial, gmlp_w_out, ffn_w1, ffn_w2, norm_final):
    bsz, s, d = x.shape
    assert bsz == 1, "the recurrence kernel carries one sequence"
    depth = norm_mix.shape[0]
    h = x.reshape(s, d).astype(F32)
    lower_bounds = _lower_bounds(hgrn_lower_bounds)
    hb, ss = _row_stats(h)

    w_in_b = _cast_fold(hgrn_w_in, 0, norm_mix[0])
    for layer in range(depth):
        j = layer // N_MIXERS
        nxt = layer + 1
        ffn_w1_side = _Side(ffn_w1, layer, norm_ffn[layer])
        if layer % N_MIXERS == 0:
            lb = lower_bounds[layer].reshape(1, d)
            q, w_out_b = _proj(hb, ss, w_in_b, n_off=0, n=d, epilogue=_ep_silu, out_dtypes=(BF16,),
                               side=_Side(hgrn_w_out, j), name="hgrn_q")
            lf, k = _proj(hb, ss, w_in_b, n_off=d, n=d, epilogue=_ep_forget, extra_rows=(lb,),
                          out_dtypes=(F32, BF16), bn=512, name="hgrn_f")
            (v,) = _proj(hb, ss, w_in_b, n_off=2 * d, n=d, epilogue=_ep_identity, out_dtypes=(BF16,), name="hgrn_i")
            (g,) = _proj(hb, ss, w_in_b, n_off=3 * d, n=d, epilogue=_ep_silu, out_dtypes=(BF16,), name="hgrn_g")
            mixed = _hgrn_recurrence(q, k, v, g, lf, hgrn_gnorm[j])
        else:
            z, w_out_b = _proj(hb, ss, w_in_b, epilogue=_ep_gelu, out_dtypes=(BF16,),
                               side=_Side(gmlp_w_out, j), name="gmlp_in")
            mixed = _gmlp_mix(z, gmlp_ln_gain[j], gmlp_ln_bias[j], gmlp_w_spatial[j], gmlp_b_spatial[j])
        h, hb, ss, w1_b = _out_proj(mixed, w_out_b, h, bn=512, bk=d, side=ffn_w1_side, name="mix_out")
        hid, w2_b = _proj(hb, ss, w1_b, epilogue=_ep_relu2, out_dtypes=(BF16,), side=_Side(ffn_w2, layer),
                          name="ffn_up")
        if nxt < depth:
            nxt_w_in = hgrn_w_in if nxt % N_MIXERS == 0 else gmlp_w_in
            h, hb, ss, w_in_b = _out_proj(hid, w2_b, h, side=_Side(nxt_w_in, nxt // N_MIXERS, norm_mix[nxt]),
                                          side_splits=2, name="ffn_down")
        else:
            h, hb, ss = _out_proj(hid, w2_b, h, name="ffn_down")
    out = _rmsnorm(h, norm_final, F32)
    return out.reshape(bsz, s, d)
```

```python
import functools
import math

import numpy as np
import jax
import jax.numpy as jnp
from jax import lax
from jax.experimental import pallas as pl
from jax.experimental.pallas import tpu as pltpu

HEAD_DIM = 128
GMLP_CHUNK = 128
N_MIXERS = 2
EPS = 1e-5
HGRN_CHUNK = 128
SUBLANES = 8
BF16_ROW_TILE = 16

V7X_VMEM_LIMIT_BYTES = 56 * 1024 * 1024

F32 = jnp.float32
BF16 = jnp.bfloat16


def _params(sem):
    return pltpu.CompilerParams(dimension_semantics=sem, vmem_limit_bytes=V7X_VMEM_LIMIT_BYTES)


def _sigmoid(x):
    return 1.0 / (1.0 + jnp.exp(-x))


def _lower_bounds_kernel(lb_ref, o_ref):
    x = lb_ref[...]
    m = jnp.max(x, axis=0, keepdims=True)
    e = jnp.exp(x - m)
    p = e / jnp.sum(e, axis=0, keepdims=True)
    depth = x.shape[0]
    run = p[0:1, :]
    rows = [run - p[0:1, :]]
    for l in range(1, depth):
        run = run + p[l:l + 1, :]
        rows.append(run - p[0:1, :])
    o_ref[...] = jnp.concatenate(rows, axis=0)


def _lower_bounds(lb):
    return pl.pallas_call(
        _lower_bounds_kernel,
        out_shape=jax.ShapeDtypeStruct(lb.shape, F32),
        name="lower_bounds",
    )(lb.astype(F32))


def _row_stats_kernel(x_ref, xb_ref, ss_ref):
    x = x_ref[...]
    xb_ref[...] = x.astype(xb_ref.dtype)
    ss_ref[...] = jnp.sum(x * x, axis=-1, keepdims=True)


def _row_stats(x):
    s, d = x.shape
    bm = min(256, s)
    return pl.pallas_call(
        _row_stats_kernel,
        grid=(s // bm,),
        in_specs=[pl.BlockSpec((bm, d), lambda i: (i, 0))],
        out_specs=(pl.BlockSpec((bm, d), lambda i: (i, 0)), pl.BlockSpec((bm, 1), lambda i: (i, 0))),
        out_shape=(jax.ShapeDtypeStruct((s, d), BF16), jax.ShapeDtypeStruct((s, 1), F32)),
        compiler_params=_params(("parallel",)),
        name="row_stats",
    )(x)


def _rmsnorm_kernel(x_ref, w_ref, o_ref):
    x = x_ref[...]
    ms = jnp.mean(x * x, axis=-1, keepdims=True)
    o_ref[...] = (x * lax.rsqrt(ms + EPS) * w_ref[...]).astype(o_ref.dtype)


def _rmsnorm(x, w, out_dtype):
    s, d = x.shape
    bm = min(256, s)
    return pl.pallas_call(
        _rmsnorm_kernel,
        grid=(s // bm,),
        in_specs=[pl.BlockSpec((bm, d), lambda i: (i, 0)),
                  pl.BlockSpec((1, d), lambda i: (0, 0))],
        out_specs=pl.BlockSpec((bm, d), lambda i: (i, 0)),
        out_shape=jax.ShapeDtypeStruct((s, d), out_dtype),
        compiler_params=_params(("parallel",)),
        name="rmsnorm",
    )(x, w.reshape(1, d).astype(F32))


def _ep_silu(acc):
    return (acc * _sigmoid(acc),)


def _ep_identity(acc):
    return (acc,)


def _ep_forget(acc, lb):
    forget = lb + (1.0 - lb) * _sigmoid(acc)
    return (jnp.log(forget), 1.0 - forget)


def _ep_gelu(acc):
    return (jax.nn.gelu(acc),)


def _ep_relu2(acc):
    r = jnp.maximum(acc, 0.0)
    return (r * r,)


def _convert_slab(w_ref, nw_ref, o_ref):
    w = w_ref[...]
    if nw_ref is not None:
        w = w * nw_ref[...]
    o_ref[...] = w.astype(o_ref.dtype)


def _cast_kernel(w_ref, nw_ref, o_ref):
    _convert_slab(w_ref, nw_ref, o_ref)


def _cast_fold(w, layer, nw):
    _, r, c = w.shape
    br = min(r, max(BF16_ROW_TILE, (2 * 1024 * 1024) // c))
    return pl.pallas_call(
        _cast_kernel,
        grid=(r // br,),
        in_specs=[pl.BlockSpec((None, br, c), lambda i: (layer, i, 0)),
                  pl.BlockSpec((br, 1), lambda i: (i, 0))],
        out_specs=pl.BlockSpec((br, c), lambda i: (i, 0)),
        out_shape=jax.ShapeDtypeStruct((r, c), BF16),
        compiler_params=_params(("parallel",)),
        name="cast_fold",
    )(w, nw.reshape(r, 1).astype(F32))


class _Side:
    def __init__(self, w, layer, nw=None):
        self.w, self.layer, self.nw = w, layer, nw

    def plan(self, n_blocks, block_of):
        _, r, c = self.w.shape
        assert r % n_blocks == 0 and (r // n_blocks) % BF16_ROW_TILE == 0, (r, n_blocks)
        br = r // n_blocks
        layer = self.layer
        in_specs = [pl.BlockSpec((None, br, c), lambda *idx: (layer, block_of(*idx), 0))]
        operands = [self.w]
        if self.nw is not None:
            in_specs.append(pl.BlockSpec((br, 1), lambda *idx: (block_of(*idx), 0)))
            operands.append(self.nw.reshape(r, 1).astype(F32))
        out_spec = pl.BlockSpec((br, c), lambda *idx: (block_of(*idx), 0))
        return in_specs, out_spec, jax.ShapeDtypeStruct((r, c), BF16), operands


def _proj_kernel(a_ref, wb_ref, ss_ref, *rest, epilogue, n_extra, n_out, inv_k, sides_nw):
    extra = rest[:n_extra]
    pos = n_extra
    side_in = []
    for has_nw in sides_nw:
        side_in.append((rest[pos], rest[pos + 1] if has_nw else None))
        pos += 2 if has_nw else 1
    outs = rest[pos:pos + n_out]
    side_out = rest[pos + n_out:]
    for (w_ref, nw_ref), o_ref in zip(side_in, side_out):
        _convert_slab(w_ref, nw_ref, o_ref)
    acc = jnp.dot(a_ref[...], wb_ref[...], preferred_element_type=F32)
    rstd = lax.rsqrt(ss_ref[...] * inv_k + EPS)
    res = epilogue(acc * rstd, *[e[...] for e in extra])
    for o_ref, r in zip(outs, res):
        o_ref[...] = r.astype(o_ref.dtype)


def _proj(hb, ss, wb, *, n_off=0, n=None, epilogue, extra_rows=(), out_dtypes, bm=1024, bn=1024, sides=(), name):
    m, kdim = hb.shape
    n = wb.shape[1] if n is None else n
    bm, bn = min(bm, m), min(bn, n)
    ni, nj = m // bm, n // bn
    off = n_off // bn
    in_specs = [pl.BlockSpec((bm, kdim), lambda i, j: (i, 0)),
                pl.BlockSpec((kdim, bn), lambda i, j: (0, j + off)),
                pl.BlockSpec((bm, 1), lambda i, j: (i, 0))]
    in_specs += [pl.BlockSpec((1, bn), lambda i, j: (0, j)) for _ in extra_rows]
    operands = [hb, wb, ss, *extra_rows]
    out_specs = [pl.BlockSpec((bm, bn), lambda i, j: (i, j)) for _ in out_dtypes]
    out_shape = [jax.ShapeDtypeStruct((m, n), dt) for dt in out_dtypes]
    for side in sides:
        s_in, s_out, s_shape, s_ops = side.plan(ni * nj, lambda i, j: i * nj + j)
        in_specs += s_in
        operands += s_ops
        out_specs.append(s_out)
        out_shape.append(s_shape)
    return pl.pallas_call(
        functools.partial(_proj_kernel, epilogue=epilogue, n_extra=len(extra_rows), n_out=len(out_dtypes),
                          inv_k=1.0 / kdim, sides_nw=tuple(side.nw is not None for side in sides)),
        grid=(ni, nj),
        in_specs=in_specs,
        out_specs=tuple(out_specs),
        out_shape=tuple(out_shape),
        compiler_params=_params(("parallel", "parallel")),
        name=name,
    )(*operands)


def _emit_residual(acc, res_ref, h_ref, hb_ref, ss_ref, first_col_block):
    h = res_ref[...] + acc
    h_ref[...] = h
    hb_ref[...] = h.astype(hb_ref.dtype)
    part = jnp.sum(h * h, axis=-1, keepdims=True)

    @pl.when(first_col_block)
    def _():
        ss_ref[...] = part

    @pl.when(jnp.logical_not(first_col_block))
    def _():
        ss_ref[...] += part


def _accumulate(acc_ref, partial_product, k, nk, emit):
    if nk == 1:
        emit(partial_product())
        return

    @pl.when(k == 0)
    def _():
        acc_ref[...] = partial_product()

    @pl.when(jnp.logical_and(k > 0, k < nk - 1))
    def _():
        acc_ref[...] += partial_product()

    @pl.when(k == nk - 1)
    def _():
        emit(acc_ref[...] + partial_product())


def _out_kernel(a_ref, wb_ref, res_ref, h_ref, hb_ref, ss_ref, acc_ref, *, nk):
    first_cols = pl.program_id(1) == 0
    _accumulate(acc_ref, lambda: jnp.dot(a_ref[...], wb_ref[...], preferred_element_type=F32),
                pl.program_id(2), nk, lambda acc: _emit_residual(acc, res_ref, h_ref, hb_ref, ss_ref, first_cols))


def _out_proj(a, wb, res, *, bm=1024, bn=1024, bk=2048, name):
    m, kdim = a.shape
    n = wb.shape[1]
    bm, bn, bk = min(bm, m), min(bn, n), min(bk, kdim)
    nk = kdim // bk
    blk = lambda i, j, k: (i, j)
    return pl.pallas_call(
        functools.partial(_out_kernel, nk=nk),
        grid=(m // bm, n // bn, nk),
        in_specs=[pl.BlockSpec((bm, bk), lambda i, j, k: (i, k)),
                  pl.BlockSpec((bk, bn), lambda i, j, k: (k, j)),
                  pl.BlockSpec((bm, bn), blk)],
        out_specs=(pl.BlockSpec((bm, bn), blk), pl.BlockSpec((bm, bn), blk),
                   pl.BlockSpec((bm, 1), lambda i, j, k: (i, 0))),
        out_shape=(jax.ShapeDtypeStruct((m, n), F32), jax.ShapeDtypeStruct((m, n), BF16),
                   jax.ShapeDtypeStruct((m, 1), F32)),
        scratch_shapes=[pltpu.VMEM((bm, bn), F32)],
        compiler_params=_params(("parallel", "arbitrary", "arbitrary")),
        name=name,
    )(a, wb, res)


MXU_LEVEL_HALF_SIZES = (4, 2)


def _hgrn_constants(c):
    n_levels = int(math.log2(c))
    t = np.arange(c)[:, None]
    j = np.arange(c)[None, :]
    blocks = [(j <= t)]
    level = np.full((c, c), -1, np.int32)
    for l in range(n_levels):
        m = c >> (l + 1)
        r = (t // (2 * m)) * (2 * m) + m - 1
        second = (t % (2 * m)) >= m
        if m in MXU_LEVEL_HALF_SIZES:
            blocks.append(np.where(second, (j > r) & (j <= t), (j > t) & (j <= r)))
        same = (t // (2 * m)) == (j // (2 * m))
        level[same & second & ((j % (2 * m)) < m)] = l
    level[np.arange(c), np.arange(c)] = n_levels
    cst = np.concatenate(blocks, axis=0).astype(np.float32)
    return np.concatenate([cst, cst], axis=1), level, n_levels


def _dot_nt(a, b):
    return lax.dot_general(a, b, (((1,), (1,)), ((), ())), preferred_element_type=F32)


def _dot_tn(a, b):
    return lax.dot_general(a, b, (((0,), (0,)), ((), ())), preferred_element_type=F32)


def _level_exponents(p, p_ref, lf, ex, c, n_levels):
    out = []
    for l in range(n_levels):
        m = c >> (l + 1)
        if m >= SUBLANES:
            pieces = []
            for b in range(c // (2 * m)):
                lo_row = 2 * m * b
                pr = p_ref[lo_row + m - 1:lo_row + m, :]
                pieces.append(pr - p[lo_row:lo_row + m])
                pieces.append(p[lo_row + m:lo_row + 2 * m] - pr)
            out.append(jnp.concatenate(pieces, axis=0))
        elif m in MXU_LEVEL_HALF_SIZES:
            i = 1 + MXU_LEVEL_HALF_SIZES.index(m)
            out.append(ex[i * c:(i + 1) * c])
        else:
            odd = (lax.broadcasted_iota(jnp.int32, lf.shape, 0) & 1) == 1
            out.append(jnp.where(odd, lf, 0.0))
    return out


def _hgrn_kernel(q_ref, k_ref, v_ref, g_ref, lf_ref, cm_ref, lv_ref, gn_ref, o_ref, st_ref, p_ref, *, c, n_levels):
    @pl.when(pl.program_id(1) == 0)
    def _():
        st_ref[...] = jnp.zeros_like(st_ref)

    t_blk, w = q_ref.shape
    n_heads = w // HEAD_DIM
    lv = lv_ref[...]
    gn = gn_ref[...]

    def chunk(ci, carry):
        r0 = pl.multiple_of(ci * c, c)
        rows = pl.ds(r0, c)
        lf = lf_ref[rows, :]
        hi = lf.astype(BF16)
        lo = (lf - hi.astype(F32)).astype(BF16)
        ex = jnp.dot(cm_ref[...], jnp.concatenate([hi, lo], axis=0), preferred_element_type=F32)
        p = ex[0:c]
        p_ref[...] = p
        dec_q = jnp.exp(p).astype(BF16)
        p_last = p_ref[c - 1:c, :]
        dec_k = jnp.exp(p_last - p).astype(BF16)
        dec_end = jnp.exp(p_last)
        dec_l = [jnp.exp(e).astype(BF16) for e in _level_exponents(p, p_ref, lf, ex, c, n_levels)]
        for h in range(n_heads):
            sl = slice(h * HEAD_DIM, (h + 1) * HEAD_DIM)
            q = q_ref[rows, sl]
            k = k_ref[rows, sl]
            v = v_ref[rows, sl]
            st = st_ref[h]
            o = _dot_nt(q * dec_q[:, sl], st.astype(BF16))
            kt = k.T
            s = jnp.where(lv == n_levels, jnp.dot(q, kt, preferred_element_type=F32), 0.0)
            for l in range(n_levels):
                d = dec_l[l][:, sl]
                s = jnp.where(lv == l, jnp.dot(q * d, kt * d.T, preferred_element_type=F32), s)
            o = o + jnp.dot(s.astype(BF16), v, preferred_element_type=F32)
            st_ref[h] = st * dec_end[:, sl] + _dot_tn(v, k * dec_k[:, sl])
            ms = jnp.mean(o * o, axis=-1, keepdims=True)
            o = o * lax.rsqrt(ms + EPS) * gn * g_ref[rows, sl].astype(F32)
            o_ref[rows, sl] = o.astype(o_ref.dtype)
        return carry

    lax.fori_loop(0, t_blk // c, chunk, 0, unroll=True)


def _hgrn_recurrence(q, k, v, g, lf, gnorm, *, c=HGRN_CHUNK, t_blk=512, w=512):
    s, d = q.shape
    c = min(c, s)
    assert c >= 2 * SUBLANES and c & (c - 1) == 0, "chunk must be a power of two of at least two sublane tiles"
    t_blk = min(t_blk, s)
    w = min(w, d)
    cm, lv, n_levels = _hgrn_constants(c)
    blk = pl.BlockSpec((t_blk, w), lambda hi, ti: (ti, hi))
    return pl.pallas_call(
        functools.partial(_hgrn_kernel, c=c, n_levels=n_levels),
        grid=(d // w, s // t_blk),
        in_specs=[blk, blk, blk, blk, blk,
                  pl.BlockSpec(cm.shape, lambda hi, ti: (0, 0)),
                  pl.BlockSpec(lv.shape, lambda hi, ti: (0, 0)),
                  pl.BlockSpec((1, HEAD_DIM), lambda hi, ti: (0, 0))],
        out_specs=blk,
        out_shape=jax.ShapeDtypeStruct((s, d), BF16),
        scratch_shapes=[pltpu.VMEM((w // HEAD_DIM, HEAD_DIM, HEAD_DIM), F32), pltpu.VMEM((c, w), F32)],
        compiler_params=_params(("parallel", "arbitrary")),
        name="hgrn_recurrence",
    )(q, k, v, g, lf, jnp.asarray(cm, BF16), jnp.asarray(lv), gnorm.reshape(1, HEAD_DIM).astype(F32))


def _gmlp_kernel(z_ref, lng_ref, lnb_ref, wsp_ref, bsp_ref, o_ref, wc_ref):
    @pl.when(pl.program_id(0) == 0)
    def _():
        n = wsp_ref.shape[1]
        tri = lax.broadcasted_iota(jnp.int32, (n, n), 0) >= lax.broadcasted_iota(jnp.int32, (n, n), 1)
        for g in range(wsp_ref.shape[0]):
            wc_ref[g] = jnp.where(tri, wsp_ref[g], 0.0).astype(BF16)

    t_blk, d2 = z_ref.shape
    d = d2 // 2
    n_groups = d // HEAD_DIM
    for ci in range(t_blk // GMLP_CHUNK):
        rows = slice(ci * GMLP_CHUNK, (ci + 1) * GMLP_CHUNK)
        v = z_ref[rows, d:].astype(F32)
        mu = jnp.mean(v, axis=-1, keepdims=True)
        vc = v - mu
        var = jnp.mean(vc * vc, axis=-1, keepdims=True)
        rstd = lax.rsqrt(var + EPS)
        for g in range(n_groups):
            sl = slice(g * HEAD_DIM, (g + 1) * HEAD_DIM)
            vn = vc[:, sl] * rstd * lng_ref[:, sl] + lnb_ref[:, sl]
            mixed = jnp.dot(wc_ref[g], vn.astype(BF16), preferred_element_type=F32) + bsp_ref[g]
            o_ref[rows, sl] = (z_ref[rows, sl].astype(F32) * mixed).astype(o_ref.dtype)


def _gmlp_mix(z, ln_g, ln_b, w_sp, b_sp, *, t_blk=256):
    s, d2 = z.shape
    d = d2 // 2
    g, n, _ = w_sp.shape
    t_blk = min(t_blk, s)
    bsp = jnp.broadcast_to(b_sp.astype(F32)[:, :, None], (g, n, HEAD_DIM))
    return pl.pallas_call(
        _gmlp_kernel,
        grid=(s // t_blk,),
        in_specs=[pl.BlockSpec((t_blk, d2), lambda i: (i, 0)),
                  pl.BlockSpec((1, d), lambda i: (0, 0)),
                  pl.BlockSpec((1, d), lambda i: (0, 0)),
                  pl.BlockSpec((g, n, n), lambda i: (0, 0, 0)),
                  pl.BlockSpec((g, n, HEAD_DIM), lambda i: (0, 0, 0))],
        out_specs=pl.BlockSpec((t_blk, d), lambda i: (i, 0)),
        out_shape=jax.ShapeDtypeStruct((s, d), BF16),
        scratch_shapes=[pltpu.VMEM((g, n, n), BF16)],
        compiler_params=_params(("arbitrary",)),
        name="gmlp_mix",
    )(z, ln_g.reshape(1, d).astype(F32), ln_b.reshape(1, d).astype(F32), w_sp.astype(F32), bsp)


def kernel(x, norm_mix, norm_ffn, hgrn_w_in, hgrn_lower_bounds, hgrn_gnorm, hgrn_w_out, gmlp_w_in,
           gmlp_ln_gain, gmlp_ln_bias, gmlp_w_spatial, gmlp_b_spatial, gmlp_w_out, ffn_w1, ffn_w2, norm_final):
    bsz, s, d = x.shape
    assert bsz == 1, "the recurrence kernel carries one sequence"
    depth = norm_mix.shape[0]
    h = x.reshape(s, d).astype(F32)
    lower_bounds = _lower_bounds(hgrn_lower_bounds)
    hb, ss = _row_stats(h)

    w_in_b = _cast_fold(hgrn_w_in, 0, norm_mix[0])
    for layer in range(depth):
        j = layer // N_MIXERS
        w1_side = _Side(ffn_w1, layer, norm_ffn[layer])
        if layer % N_MIXERS == 0:
            lb = lower_bounds[layer].reshape(1, d)
            q, w_out_b = _proj(hb, ss, w_in_b, n_off=0, n=d, epilogue=_ep_silu, out_dtypes=(BF16,),
                               sides=(_Side(hgrn_w_out, j),), name="hgrn_q")
            lf, k = _proj(hb, ss, w_in_b, n_off=d, n=d, epilogue=_ep_forget, extra_rows=(lb,),
                          out_dtypes=(F32, BF16), bn=512, name="hgrn_f")
            v, w1_b = _proj(hb, ss, w_in_b, n_off=2 * d, n=d, epilogue=_ep_identity, out_dtypes=(BF16,),
                            sides=(w1_side,), bn=512, name="hgrn_i")
            (g,) = _proj(hb, ss, w_in_b, n_off=3 * d, n=d, epilogue=_ep_silu, out_dtypes=(BF16,), name="hgrn_g")
            mixed = _hgrn_recurrence(q, k, v, g, lf, hgrn_gnorm[j])
        else:
            z, w_out_b, w1_b = _proj(hb, ss, w_in_b, epilogue=_ep_gelu, out_dtypes=(BF16,),
                                     sides=(_Side(gmlp_w_out, j), w1_side), name="gmlp_in")
            mixed = _gmlp_mix(z, gmlp_ln_gain[j], gmlp_ln_bias[j], gmlp_w_spatial[j], gmlp_b_spatial[j])
        h, hb, ss = _out_proj(mixed, w_out_b, h, bn=512, bk=d, name="mix_out")
        up_sides = [_Side(ffn_w2, layer)]
        nxt = layer + 1
        if nxt < depth:
            nxt_w_in = hgrn_w_in if nxt % N_MIXERS == 0 else gmlp_w_in
            up_sides.append(_Side(nxt_w_in, nxt // N_MIXERS, norm_mix[nxt]))
        hid, w2_b, *nxt_w = _proj(hb, ss, w1_b, epilogue=_ep_relu2, out_dtypes=(BF16,), sides=tuple(up_sides),
                                  name="ffn_up")
        if nxt_w:
            w_in_b = nxt_w[0]
        h, hb, ss = _out_proj(hid, w2_b, h, name="ffn_down")
    out = _rmsnorm(h, norm_final, F32)
    return out.reshape(bsz, s, d)
```

```python
import functools
import math

import numpy as np
import jax
import jax.numpy as jnp
from jax import lax
from jax.experimental import pallas as pl
from jax.experimental.pallas import tpu as pltpu

HEAD_DIM = 128
GMLP_CHUNK = 128
N_MIXERS = 2
EPS = 1e-5
HGRN_CHUNK = 128
SUBLANES = 8
BF16_ROW_TILE = 16

V7X_VMEM_LIMIT_BYTES = 56 * 1024 * 1024

F32 = jnp.float32
BF16 = jnp.bfloat16


def _params(sem):
    return pltpu.CompilerParams(dimension_semantics=sem, vmem_limit_bytes=V7X_VMEM_LIMIT_BYTES)


def _sigmoid(x):
    return 1.0 / (1.0 + jnp.exp(-x))


def _lower_bounds_kernel(lb_ref, o_ref):
    x = lb_ref[...]
    m = jnp.max(x, axis=0, keepdims=True)
    e = jnp.exp(x - m)
    p = e / jnp.sum(e, axis=0, keepdims=True)
    depth = x.shape[0]
    run = p[0:1, :]
    rows = [run - p[0:1, :]]
    for l in range(1, depth):
        run = run + p[l:l + 1, :]
        rows.append(run - p[0:1, :])
    o_ref[...] = jnp.concatenate(rows, axis=0)


def _lower_bounds(lb):
    return pl.pallas_call(
        _lower_bounds_kernel,
        out_shape=jax.ShapeDtypeStruct(lb.shape, F32),
        name="lower_bounds",
    )(lb.astype(F32))


def _row_stats_kernel(x_ref, xb_ref, ss_ref):
    x = x_ref[...]
    xb_ref[...] = x.astype(xb_ref.dtype)
    ss_ref[...] = jnp.sum(x * x, axis=-1, keepdims=True)


def _row_stats(x):
    s, d = x.shape
    bm = min(256, s)
    return pl.pallas_call(
        _row_stats_kernel,
        grid=(s // bm,),
        in_specs=[pl.BlockSpec((bm, d), lambda i: (i, 0))],
        out_specs=(pl.BlockSpec((bm, d), lambda i: (i, 0)), pl.BlockSpec((bm, 1), lambda i: (i, 0))),
        out_shape=(jax.ShapeDtypeStruct((s, d), BF16), jax.ShapeDtypeStruct((s, 1), F32)),
        compiler_params=_params(("parallel",)),
        name="row_stats",
    )(x)


def _rmsnorm_kernel(x_ref, w_ref, o_ref):
    x = x_ref[...]
    ms = jnp.mean(x * x, axis=-1, keepdims=True)
    o_ref[...] = (x * lax.rsqrt(ms + EPS) * w_ref[...]).astype(o_ref.dtype)


def _rmsnorm(x, w, out_dtype):
    s, d = x.shape
    bm = min(256, s)
    return pl.pallas_call(
        _rmsnorm_kernel,
        grid=(s // bm,),
        in_specs=[pl.BlockSpec((bm, d), lambda i: (i, 0)),
                  pl.BlockSpec((1, d), lambda i: (0, 0))],
        out_specs=pl.BlockSpec((bm, d), lambda i: (i, 0)),
        out_shape=jax.ShapeDtypeStruct((s, d), out_dtype),
        compiler_params=_params(("parallel",)),
        name="rmsnorm",
    )(x, w.reshape(1, d).astype(F32))


def _ep_silu(acc):
    return (acc * _sigmoid(acc),)


def _ep_identity(acc):
    return (acc,)


def _ep_forget(acc, lb):
    forget = lb + (1.0 - lb) * _sigmoid(acc)
    return (jnp.log(forget), 1.0 - forget)


def _ep_gelu(acc):
    return (jax.nn.gelu(acc),)


def _ep_relu2(acc):
    r = jnp.maximum(acc, 0.0)
    return (r * r,)


def _convert_slab(w_ref, nw_ref, o_ref):
    w = w_ref[...]
    if nw_ref is not None:
        w = w * nw_ref[...]
    o_ref[...] = w.astype(o_ref.dtype)


def _cast_kernel(w_ref, nw_ref, o_ref):
    _convert_slab(w_ref, nw_ref, o_ref)


def _cast_fold(w, layer, nw):
    _, r, c = w.shape
    br = min(r, max(BF16_ROW_TILE, (2 * 1024 * 1024) // c))
    return pl.pallas_call(
        _cast_kernel,
        grid=(r // br,),
        in_specs=[pl.BlockSpec((None, br, c), lambda i: (layer, i, 0)),
                  pl.BlockSpec((br, 1), lambda i: (i, 0))],
        out_specs=pl.BlockSpec((br, c), lambda i: (i, 0)),
        out_shape=jax.ShapeDtypeStruct((r, c), BF16),
        compiler_params=_params(("parallel",)),
        name="cast_fold",
    )(w, nw.reshape(r, 1).astype(F32))


class _Side:
    def __init__(self, w, layer, nw=None):
        self.w, self.layer, self.nw = w, layer, nw

    def plan(self, n_blocks, block_of):
        _, r, c = self.w.shape
        assert r % n_blocks == 0 and (r // n_blocks) % BF16_ROW_TILE == 0, (r, n_blocks)
        br = r // n_blocks
        layer = self.layer
        in_specs = [pl.BlockSpec((None, br, c), lambda *idx: (layer, block_of(*idx), 0))]
        operands = [self.w]
        if self.nw is not None:
            in_specs.append(pl.BlockSpec((br, 1), lambda *idx: (block_of(*idx), 0)))
            operands.append(self.nw.reshape(r, 1).astype(F32))
        out_spec = pl.BlockSpec((br, c), lambda *idx: (block_of(*idx), 0))
        return in_specs, out_spec, jax.ShapeDtypeStruct((r, c), BF16), operands


def _proj_kernel(a_ref, wb_ref, ss_ref, *rest, epilogue, n_extra, n_out, inv_k, sides_nw):
    extra = rest[:n_extra]
    pos = n_extra
    side_in = []
    for has_nw in sides_nw:
        side_in.append((rest[pos], rest[pos + 1] if has_nw else None))
        pos += 2 if has_nw else 1
    outs = rest[pos:pos + n_out]
    side_out = rest[pos + n_out:]
    for (w_ref, nw_ref), o_ref in zip(side_in, side_out):
        _convert_slab(w_ref, nw_ref, o_ref)
    acc = jnp.dot(a_ref[...], wb_ref[...], preferred_element_type=F32)
    rstd = lax.rsqrt(ss_ref[...] * inv_k + EPS)
    res = epilogue(acc * rstd, *[e[...] for e in extra])
    for o_ref, r in zip(outs, res):
        o_ref[...] = r.astype(o_ref.dtype)


def _proj(hb, ss, wb, *, n_off=0, n=None, epilogue, extra_rows=(), out_dtypes, bm=1024, bn=1024, sides=(), name):
    m, kdim = hb.shape
    n = wb.shape[1] if n is None else n
    bm, bn = min(bm, m), min(bn, n)
    ni, nj = m // bm, n // bn
    off = n_off // bn
    in_specs = [pl.BlockSpec((bm, kdim), lambda i, j: (i, 0)),
                pl.BlockSpec((kdim, bn), lambda i, j: (0, j + off)),
                pl.BlockSpec((bm, 1), lambda i, j: (i, 0))]
    in_specs += [pl.BlockSpec((1, bn), lambda i, j: (0, j)) for _ in extra_rows]
    operands = [hb, wb, ss, *extra_rows]
    out_specs = [pl.BlockSpec((bm, bn), lambda i, j: (i, j)) for _ in out_dtypes]
    out_shape = [jax.ShapeDtypeStruct((m, n), dt) for dt in out_dtypes]
    for side in sides:
        s_in, s_out, s_shape, s_ops = side.plan(ni * nj, lambda i, j: i * nj + j)
        in_specs += s_in
        operands += s_ops
        out_specs.append(s_out)
        out_shape.append(s_shape)
    return pl.pallas_call(
        functools.partial(_proj_kernel, epilogue=epilogue, n_extra=len(extra_rows), n_out=len(out_dtypes),
                          inv_k=1.0 / kdim, sides_nw=tuple(side.nw is not None for side in sides)),
        grid=(ni, nj),
        in_specs=in_specs,
        out_specs=tuple(out_specs),
        out_shape=tuple(out_shape),
        compiler_params=_params(("parallel", "parallel")),
        name=name,
    )(*operands)


def _emit_residual(acc, res_ref, h_ref, hb_ref, ss_ref, first_col_block):
    h = res_ref[...] + acc
    h_ref[...] = h
    hb_ref[...] = h.astype(hb_ref.dtype)
    part = jnp.sum(h * h, axis=-1, keepdims=True)

    @pl.when(first_col_block)
    def _():
        ss_ref[...] = part

    @pl.when(jnp.logical_not(first_col_block))
    def _():
        ss_ref[...] += part


def _accumulate(acc_ref, partial_product, k, nk, emit):
    if nk == 1:
        emit(partial_product())
        return

    @pl.when(k == 0)
    def _():
        acc_ref[...] = partial_product()

    @pl.when(jnp.logical_and(k > 0, k < nk - 1))
    def _():
        acc_ref[...] += partial_product()

    @pl.when(k == nk - 1)
    def _():
        emit(acc_ref[...] + partial_product())


def _out_kernel(a_ref, wb_ref, res_ref, h_ref, hb_ref, ss_ref, acc_ref, *, nk):
    first_cols = pl.program_id(1) == 0
    _accumulate(acc_ref, lambda: jnp.dot(a_ref[...], wb_ref[...], preferred_element_type=F32),
                pl.program_id(2), nk, lambda acc: _emit_residual(acc, res_ref, h_ref, hb_ref, ss_ref, first_cols))


def _out_proj(a, wb, res, *, bm=1024, bn=1024, bk=2048, name):
    m, kdim = a.shape
    n = wb.shape[1]
    bm, bn, bk = min(bm, m), min(bn, n), min(bk, kdim)
    nk = kdim // bk
    blk = lambda i, j, k: (i, j)
    return pl.pallas_call(
        functools.partial(_out_kernel, nk=nk),
        grid=(m // bm, n // bn, nk),
        in_specs=[pl.BlockSpec((bm, bk), lambda i, j, k: (i, k)),
                  pl.BlockSpec((bk, bn), lambda i, j, k: (k, j)),
                  pl.BlockSpec((bm, bn), blk)],
        out_specs=(pl.BlockSpec((bm, bn), blk), pl.BlockSpec((bm, bn), blk),
                   pl.BlockSpec((bm, 1), lambda i, j, k: (i, 0))),
        out_shape=(jax.ShapeDtypeStruct((m, n), F32), jax.ShapeDtypeStruct((m, n), BF16),
                   jax.ShapeDtypeStruct((m, 1), F32)),
        scratch_shapes=[pltpu.VMEM((bm, bn), F32)],
        compiler_params=_params(("parallel", "arbitrary", "arbitrary")),
        name=name,
    )(a, wb, res)


MXU_LEVEL_HALF_SIZES = (4, 2)


def _hgrn_constants(c):
    n_levels = int(math.log2(c))
    t = np.arange(c)[:, None]
    j = np.arange(c)[None, :]
    blocks = [(j <= t)]
    level = np.full((c, c), -1, np.int32)
    for l in range(n_levels):
        m = c >> (l + 1)
        r = (t // (2 * m)) * (2 * m) + m - 1
        second = (t % (2 * m)) >= m
        if m in MXU_LEVEL_HALF_SIZES:
            blocks.append(np.where(second, (j > r) & (j <= t), (j > t) & (j <= r)))
        same = (t // (2 * m)) == (j // (2 * m))
        level[same & second & ((j % (2 * m)) < m)] = l
    level[np.arange(c), np.arange(c)] = n_levels
    cst = np.concatenate(blocks, axis=0).astype(np.float32)
    return np.concatenate([cst, cst], axis=1), level, n_levels


def _dot_nt(a, b):
    return lax.dot_general(a, b, (((1,), (1,)), ((), ())), preferred_element_type=F32)


def _dot_tn(a, b):
    return lax.dot_general(a, b, (((0,), (0,)), ((), ())), preferred_element_type=F32)


def _level_exponents(p, p_ref, lf, ex, c, n_levels):
    out = []
    for l in range(n_levels):
        m = c >> (l + 1)
        if m >= SUBLANES:
            pieces = []
            for b in range(c // (2 * m)):
                lo_row = 2 * m * b
                pr = p_ref[lo_row + m - 1:lo_row + m, :]
                pieces.append(pr - p[lo_row:lo_row + m])
                pieces.append(p[lo_row + m:lo_row + 2 * m] - pr)
            out.append(jnp.concatenate(pieces, axis=0))
        elif m in MXU_LEVEL_HALF_SIZES:
            i = 1 + MXU_LEVEL_HALF_SIZES.index(m)
            out.append(ex[i * c:(i + 1) * c])
        else:
            odd = (lax.broadcasted_iota(jnp.int32, lf.shape, 0) & 1) == 1
            out.append(jnp.where(odd, lf, 0.0))
    return out


def _hgrn_kernel(q_ref, k_ref, v_ref, g_ref, lf_ref, cm_ref, lv_ref, gn_ref, o_ref, st_ref, p_ref, *, c, n_levels):
    @pl.when(pl.program_id(1) == 0)
    def _():
        st_ref[...] = jnp.zeros_like(st_ref)

    t_blk, w = q_ref.shape
    n_heads = w // HEAD_DIM
    lv = lv_ref[...]
    gn = gn_ref[...]

    def chunk(ci, carry):
        r0 = pl.multiple_of(ci * c, c)
        rows = pl.ds(r0, c)
        lf = lf_ref[rows, :]
        hi = lf.astype(BF16)
        lo = (lf - hi.astype(F32)).astype(BF16)
        ex = jnp.dot(cm_ref[...], jnp.concatenate([hi, lo], axis=0), preferred_element_type=F32)
        p = ex[0:c]
        p_ref[...] = p
        dec_q = jnp.exp(p).astype(BF16)
        p_last = p_ref[c - 1:c, :]
        dec_k = jnp.exp(p_last - p).astype(BF16)
        dec_end = jnp.exp(p_last)
        dec_l = [jnp.exp(e).astype(BF16) for e in _level_exponents(p, p_ref, lf, ex, c, n_levels)]
        for h in range(n_heads):
            sl = slice(h * HEAD_DIM, (h + 1) * HEAD_DIM)
            q = q_ref[rows, sl]
            k = k_ref[rows, sl]
            v = v_ref[rows, sl]
            st = st_ref[h]
            o = _dot_nt(q * dec_q[:, sl], st.astype(BF16))
            kt = k.T
            s = jnp.where(lv == n_levels, jnp.dot(q, kt, preferred_element_type=F32), 0.0)
            for l in range(n_levels):
                d = dec_l[l][:, sl]
                s = jnp.where(lv == l, jnp.dot(q * d, kt * d.T, preferred_element_type=F32), s)
            o = o + jnp.dot(s.astype(BF16), v, preferred_element_type=F32)
            st_ref[h] = st * dec_end[:, sl] + _dot_tn(v, k * dec_k[:, sl])
            ms = jnp.mean(o * o, axis=-1, keepdims=True)
            o = o * lax.rsqrt(ms + EPS) * gn * g_ref[rows, sl].astype(F32)
            o_ref[rows, sl] = o.astype(o_ref.dtype)
        return carry

    lax.fori_loop(0, t_blk // c, chunk, 0, unroll=True)


def _hgrn_recurrence(q, k, v, g, lf, gnorm, *, c=HGRN_CHUNK, t_blk=1024, w=512):
    s, d = q.shape
    c = min(c, s)
    assert c >= 2 * SUBLANES and c & (c - 1) == 0, "chunk must be a power of two of at least two sublane tiles"
    t_blk = min(t_blk, s)
    w = min(w, d)
    cm, lv, n_levels = _hgrn_constants(c)
    blk = pl.BlockSpec((t_blk, w), lambda hi, ti: (ti, hi))
    return pl.pallas_call(
        functools.partial(_hgrn_kernel, c=c, n_levels=n_levels),
        grid=(d // w, s // t_blk),
        in_specs=[blk, blk, blk, blk, blk,
                  pl.BlockSpec(cm.shape, lambda hi, ti: (0, 0)),
                  pl.BlockSpec(lv.shape, lambda hi, ti: (0, 0)),
                  pl.BlockSpec((1, HEAD_DIM), lambda hi, ti: (0, 0))],
        out_specs=blk,
        out_shape=jax.ShapeDtypeStruct((s, d), BF16),
        scratch_shapes=[pltpu.VMEM((w // HEAD_DIM, HEAD_DIM, HEAD_DIM), F32), pltpu.VMEM((c, w), F32)],
        compiler_params=_params(("parallel", "arbitrary")),
        name="hgrn_recurrence",
    )(q, k, v, g, lf, jnp.asarray(cm, BF16), jnp.asarray(lv), gnorm.reshape(1, HEAD_DIM).astype(F32))


def _gmlp_kernel(z_ref, lng_ref, lnb_ref, wsp_ref, bsp_ref, o_ref, wc_ref):
    @pl.when(pl.program_id(0) == 0)
    def _():
        n = wsp_ref.shape[1]
        tri = lax.broadcasted_iota(jnp.int32, (n, n), 0) >= lax.broadcasted_iota(jnp.int32, (n, n), 1)
        for g in range(wsp_ref.shape[0]):
            wc_ref[g] = jnp.where(tri, wsp_ref[g], 0.0).astype(BF16)

    t_blk, d2 = z_ref.shape
    d = d2 // 2
    n_groups = d // HEAD_DIM
    for ci in range(t_blk // GMLP_CHUNK):
        rows = slice(ci * GMLP_CHUNK, (ci + 1) * GMLP_CHUNK)
        v = z_ref[rows, d:].astype(F32)
        mu = jnp.mean(v, axis=-1, keepdims=True)
        vc = v - mu
        var = jnp.mean(vc * vc, axis=-1, keepdims=True)
        rstd = lax.rsqrt(var + EPS)
        for g in range(n_groups):
            sl = slice(g * HEAD_DIM, (g + 1) * HEAD_DIM)
            vn = vc[:, sl] * rstd * lng_ref[:, sl] + lnb_ref[:, sl]
            mixed = jnp.dot(wc_ref[g], vn.astype(BF16), preferred_element_type=F32) + bsp_ref[g]
            o_ref[rows, sl] = (z_ref[rows, sl].astype(F32) * mixed).astype(o_ref.dtype)


def _gmlp_mix(z, ln_g, ln_b, w_sp, b_sp, *, t_blk=256):
    s, d2 = z.shape
    d = d2 // 2
    g, n, _ = w_sp.shape
    t_blk = min(t_blk, s)
    bsp = jnp.broadcast_to(b_sp.astype(F32)[:, :, None], (g, n, HEAD_DIM))
    return pl.pallas_call(
        _gmlp_kernel,
        grid=(s // t_blk,),
        in_specs=[pl.BlockSpec((t_blk, d2), lambda i: (i, 0)),
                  pl.BlockSpec((1, d), lambda i: (0, 0)),
                  pl.BlockSpec((1, d), lambda i: (0, 0)),
                  pl.BlockSpec((g, n, n), lambda i: (0, 0, 0)),
                  pl.BlockSpec((g, n, HEAD_DIM), lambda i: (0, 0, 0))],
        out_specs=pl.BlockSpec((t_blk, d), lambda i: (i, 0)),
        out_shape=jax.ShapeDtypeStruct((s, d), BF16),
        scratch_shapes=[pltpu.VMEM((g, n, n), BF16)],
        compiler_params=_params(("arbitrary",)),
        name="gmlp_mix",
    )(z, ln_g.reshape(1, d).astype(F32), ln_b.reshape(1, d).astype(F32), w_sp.astype(F32), bsp)


def kernel(x, norm_mix, norm_ffn, hgrn_w_in, hgrn_lower_bounds, hgrn_gnorm, hgrn_w_out, gmlp_w_in,
           gmlp_ln_gain, gmlp_ln_bias, gmlp_w_spatial, gmlp_b_spatial, gmlp_w_out, ffn_w1, ffn_w2, norm_final):
    bsz, s, d = x.shape
    assert bsz == 1, "the recurrence kernel carries one sequence"
    depth = norm_mix.shape[0]
    h = x.reshape(s, d).astype(F32)
    lower_bounds = _lower_bounds(hgrn_lower_bounds)
    hb, ss = _row_stats(h)

    w_in_b = _cast_fold(hgrn_w_in, 0, norm_mix[0])
    for layer in range(depth):
        j = layer // N_MIXERS
        w1_side = _Side(ffn_w1, layer, norm_ffn[layer])
        if layer % N_MIXERS == 0:
            lb = lower_bounds[layer].reshape(1, d)
            q, w_out_b = _proj(hb, ss, w_in_b, n_off=0, n=d, epilogue=_ep_silu, out_dtypes=(BF16,),
                               sides=(_Side(hgrn_w_out, j),), name="hgrn_q")
            lf, k = _proj(hb, ss, w_in_b, n_off=d, n=d, epilogue=_ep_forget, extra_rows=(lb,),
                          out_dtypes=(F32, BF16), name="hgrn_f")
            v, w1_b = _proj(hb, ss, w_in_b, n_off=2 * d, n=d, epilogue=_ep_identity, out_dtypes=(BF16,),
                            sides=(w1_side,), bn=512, name="hgrn_i")
            (g,) = _proj(hb, ss, w_in_b, n_off=3 * d, n=d, epilogue=_ep_silu, out_dtypes=(BF16,), name="hgrn_g")
            mixed = _hgrn_recurrence(q, k, v, g, lf, hgrn_gnorm[j])
        else:
            z, w_out_b, w1_b = _proj(hb, ss, w_in_b, epilogue=_ep_gelu, out_dtypes=(BF16,),
                                     sides=(_Side(gmlp_w_out, j), w1_side), name="gmlp_in")
            mixed = _gmlp_mix(z, gmlp_ln_gain[j], gmlp_ln_bias[j], gmlp_w_spatial[j], gmlp_b_spatial[j])
        h, hb, ss = _out_proj(mixed, w_out_b, h, bn=512, bk=d, name="mix_out")
        up_sides = [_Side(ffn_w2, layer)]
        nxt = layer + 1
        if nxt < depth:
            nxt_w_in = hgrn_w_in if nxt % N_MIXERS == 0 else gmlp_w_in
            up_sides.append(_Side(nxt_w_in, nxt // N_MIXERS, norm_mix[nxt]))
        hid, w2_b, *nxt_w = _proj(hb, ss, w1_b, epilogue=_ep_relu2, out_dtypes=(BF16,), sides=tuple(up_sides),
                                  name="ffn_up")
        if nxt_w:
            w_in_b = nxt_w[0]
        h, hb, ss = _out_proj(hid, w2_b, h, name="ffn_down")
    out = _rmsnorm(h, norm_final, F32)
    return out.reshape(bsz, s, d)
```

```python
import functools
import math

import numpy as np
import jax
import jax.numpy as jnp
from jax import lax
from jax.experimental import pallas as pl
from jax.experimental.pallas import tpu as pltpu

HEAD_DIM = 128
GMLP_CHUNK = 128
N_MIXERS = 2
EPS = 1e-5
HGRN_CHUNK = 128
SUBLANES = 8
BF16_ROW_TILE = 16

V7X_VMEM_LIMIT_BYTES = 56 * 1024 * 1024

F32 = jnp.float32
BF16 = jnp.bfloat16


def _params(sem):
    return pltpu.CompilerParams(dimension_semantics=sem, vmem_limit_bytes=V7X_VMEM_LIMIT_BYTES)


def _sigmoid(x):
    return 1.0 / (1.0 + jnp.exp(-x))


def _lower_bounds_kernel(lb_ref, o_ref):
    x = lb_ref[...]
    m = jnp.max(x, axis=0, keepdims=True)
    e = jnp.exp(x - m)
    p = e / jnp.sum(e, axis=0, keepdims=True)
    depth = x.shape[0]
    run = p[0:1, :]
    rows = [run - p[0:1, :]]
    for l in range(1, depth):
        run = run + p[l:l + 1, :]
        rows.append(run - p[0:1, :])
    o_ref[...] = jnp.concatenate(rows, axis=0)


def _lower_bounds(lb):
    return pl.pallas_call(
        _lower_bounds_kernel,
        out_shape=jax.ShapeDtypeStruct(lb.shape, F32),
        name="lower_bounds",
    )(lb.astype(F32))


def _row_stats_kernel(x_ref, xb_ref, ss_ref):
    x = x_ref[...]
    xb_ref[...] = x.astype(xb_ref.dtype)
    ss_ref[...] = jnp.sum(x * x, axis=-1, keepdims=True)


def _row_stats(x):
    s, d = x.shape
    bm = min(256, s)
    return pl.pallas_call(
        _row_stats_kernel,
        grid=(s // bm,),
        in_specs=[pl.BlockSpec((bm, d), lambda i: (i, 0))],
        out_specs=(pl.BlockSpec((bm, d), lambda i: (i, 0)), pl.BlockSpec((bm, 1), lambda i: (i, 0))),
        out_shape=(jax.ShapeDtypeStruct((s, d), BF16), jax.ShapeDtypeStruct((s, 1), F32)),
        compiler_params=_params(("parallel",)),
        name="row_stats",
    )(x)


def _rmsnorm_kernel(x_ref, w_ref, o_ref):
    x = x_ref[...]
    ms = jnp.mean(x * x, axis=-1, keepdims=True)
    o_ref[...] = (x * lax.rsqrt(ms + EPS) * w_ref[...]).astype(o_ref.dtype)


def _rmsnorm(x, w, out_dtype):
    s, d = x.shape
    bm = min(256, s)
    return pl.pallas_call(
        _rmsnorm_kernel,
        grid=(s // bm,),
        in_specs=[pl.BlockSpec((bm, d), lambda i: (i, 0)),
                  pl.BlockSpec((1, d), lambda i: (0, 0))],
        out_specs=pl.BlockSpec((bm, d), lambda i: (i, 0)),
        out_shape=jax.ShapeDtypeStruct((s, d), out_dtype),
        compiler_params=_params(("parallel",)),
        name="rmsnorm",
    )(x, w.reshape(1, d).astype(F32))


def _ep_silu(acc):
    return (acc * _sigmoid(acc),)


def _ep_identity(acc):
    return (acc,)


def _ep_forget(acc, lb):
    forget = lb + (1.0 - lb) * _sigmoid(acc)
    return (jnp.log(forget), 1.0 - forget)


def _ep_gelu(acc):
    return (jax.nn.gelu(acc),)


def _ep_relu2(acc):
    r = jnp.maximum(acc, 0.0)
    return (r * r,)


def _convert_slab(w_ref, nw_ref, o_ref):
    w = w_ref[...]
    if nw_ref is not None:
        w = w * nw_ref[...]
    o_ref[...] = w.astype(o_ref.dtype)


def _cast_kernel(w_ref, nw_ref, o_ref):
    _convert_slab(w_ref, nw_ref, o_ref)


def _cast_fold(w, layer, nw):
    _, r, c = w.shape
    br = min(r, max(BF16_ROW_TILE, (2 * 1024 * 1024) // c))
    return pl.pallas_call(
        _cast_kernel,
        grid=(r // br,),
        in_specs=[pl.BlockSpec((None, br, c), lambda i: (layer, i, 0)),
                  pl.BlockSpec((br, 1), lambda i: (i, 0))],
        out_specs=pl.BlockSpec((br, c), lambda i: (i, 0)),
        out_shape=jax.ShapeDtypeStruct((r, c), BF16),
        compiler_params=_params(("parallel",)),
        name="cast_fold",
    )(w, nw.reshape(r, 1).astype(F32))


class _Side:
    def __init__(self, w, layer, nw=None):
        self.w, self.layer, self.nw = w, layer, nw

    def plan(self, n_blocks, block_of):
        _, r, c = self.w.shape
        assert r % n_blocks == 0 and (r // n_blocks) % BF16_ROW_TILE == 0, (r, n_blocks)
        br = r // n_blocks
        layer = self.layer
        in_specs = [pl.BlockSpec((None, br, c), lambda *idx: (layer, block_of(*idx), 0))]
        operands = [self.w]
        if self.nw is not None:
            in_specs.append(pl.BlockSpec((br, 1), lambda *idx: (block_of(*idx), 0)))
            operands.append(self.nw.reshape(r, 1).astype(F32))
        out_spec = pl.BlockSpec((br, c), lambda *idx: (block_of(*idx), 0))
        return in_specs, out_spec, jax.ShapeDtypeStruct((r, c), BF16), operands


def _proj_kernel(a_ref, wb_ref, ss_ref, *rest, epilogue, n_extra, n_out, inv_k, sides_nw):
    extra = rest[:n_extra]
    pos = n_extra
    side_in = []
    for has_nw in sides_nw:
        side_in.append((rest[pos], rest[pos + 1] if has_nw else None))
        pos += 2 if has_nw else 1
    outs = rest[pos:pos + n_out]
    side_out = rest[pos + n_out:]
    for (w_ref, nw_ref), o_ref in zip(side_in, side_out):
        _convert_slab(w_ref, nw_ref, o_ref)
    acc = jnp.dot(a_ref[...], wb_ref[...], preferred_element_type=F32)
    rstd = lax.rsqrt(ss_ref[...] * inv_k + EPS)
    res = epilogue(acc * rstd, *[e[...] for e in extra])
    for o_ref, r in zip(outs, res):
        o_ref[...] = r.astype(o_ref.dtype)


def _proj(hb, ss, wb, *, n_off=0, n=None, epilogue, extra_rows=(), out_dtypes, bm=1024, bn=1024, sides=(), name):
    m, kdim = hb.shape
    n = wb.shape[1] if n is None else n
    bm, bn = min(bm, m), min(bn, n)
    ni, nj = m // bm, n // bn
    off = n_off // bn
    in_specs = [pl.BlockSpec((bm, kdim), lambda i, j: (i, 0)),
                pl.BlockSpec((kdim, bn), lambda i, j: (0, j + off)),
                pl.BlockSpec((bm, 1), lambda i, j: (i, 0))]
    in_specs += [pl.BlockSpec((1, bn), lambda i, j: (0, j)) for _ in extra_rows]
    operands = [hb, wb, ss, *extra_rows]
    out_specs = [pl.BlockSpec((bm, bn), lambda i, j: (i, j)) for _ in out_dtypes]
    out_shape = [jax.ShapeDtypeStruct((m, n), dt) for dt in out_dtypes]
    for side in sides:
        s_in, s_out, s_shape, s_ops = side.plan(ni * nj, lambda i, j: i * nj + j)
        in_specs += s_in
        operands += s_ops
        out_specs.append(s_out)
        out_shape.append(s_shape)
    return pl.pallas_call(
        functools.partial(_proj_kernel, epilogue=epilogue, n_extra=len(extra_rows), n_out=len(out_dtypes),
                          inv_k=1.0 / kdim, sides_nw=tuple(side.nw is not None for side in sides)),
        grid=(ni, nj),
        in_specs=in_specs,
        out_specs=tuple(out_specs),
        out_shape=tuple(out_shape),
        compiler_params=_params(("parallel", "parallel")),
        name=name,
    )(*operands)


def _emit_residual(acc, res_ref, h_ref, hb_ref, ss_ref, first_col_block):
    h = res_ref[...] + acc
    h_ref[...] = h
    hb_ref[...] = h.astype(hb_ref.dtype)
    part = jnp.sum(h * h, axis=-1, keepdims=True)

    @pl.when(first_col_block)
    def _():
        ss_ref[...] = part

    @pl.when(jnp.logical_not(first_col_block))
    def _():
        ss_ref[...] += part


def _accumulate(acc_ref, partial_product, k, nk, emit):
    if nk == 1:
        emit(partial_product())
        return

    @pl.when(k == 0)
    def _():
        acc_ref[...] = partial_product()

    @pl.when(jnp.logical_and(k > 0, k < nk - 1))
    def _():
        acc_ref[...] += partial_product()

    @pl.when(k == nk - 1)
    def _():
        emit(acc_ref[...] + partial_product())


def _out_kernel(a_ref, wb_ref, res_ref, h_ref, hb_ref, ss_ref, acc_ref, *, nk):
    first_cols = pl.program_id(1) == 0
    _accumulate(acc_ref, lambda: jnp.dot(a_ref[...], wb_ref[...], preferred_element_type=F32),
                pl.program_id(2), nk, lambda acc: _emit_residual(acc, res_ref, h_ref, hb_ref, ss_ref, first_cols))


def _out_proj(a, wb, res, *, bm=1024, bn=1024, bk=2048, name):
    m, kdim = a.shape
    n = wb.shape[1]
    bm, bn, bk = min(bm, m), min(bn, n), min(bk, kdim)
    nk = kdim // bk
    blk = lambda i, j, k: (i, j)
    return pl.pallas_call(
        functools.partial(_out_kernel, nk=nk),
        grid=(m // bm, n // bn, nk),
        in_specs=[pl.BlockSpec((bm, bk), lambda i, j, k: (i, k)),
                  pl.BlockSpec((bk, bn), lambda i, j, k: (k, j)),
                  pl.BlockSpec((bm, bn), blk)],
        out_specs=(pl.BlockSpec((bm, bn), blk), pl.BlockSpec((bm, bn), blk),
                   pl.BlockSpec((bm, 1), lambda i, j, k: (i, 0))),
        out_shape=(jax.ShapeDtypeStruct((m, n), F32), jax.ShapeDtypeStruct((m, n), BF16),
                   jax.ShapeDtypeStruct((m, 1), F32)),
        scratch_shapes=[pltpu.VMEM((bm, bn), F32)],
        compiler_params=_params(("parallel", "arbitrary", "arbitrary")),
        name=name,
    )(a, wb, res)


MXU_LEVEL_HALF_SIZES = (4, 2)


def _hgrn_constants(c):
    n_levels = int(math.log2(c))
    t = np.arange(c)[:, None]
    j = np.arange(c)[None, :]
    blocks = [(j <= t)]
    level = np.full((c, c), -1, np.int32)
    for l in range(n_levels):
        m = c >> (l + 1)
        r = (t // (2 * m)) * (2 * m) + m - 1
        second = (t % (2 * m)) >= m
        if m in MXU_LEVEL_HALF_SIZES:
            blocks.append(np.where(second, (j > r) & (j <= t), (j > t) & (j <= r)))
        same = (t // (2 * m)) == (j // (2 * m))
        level[same & second & ((j % (2 * m)) < m)] = l
    level[np.arange(c), np.arange(c)] = n_levels
    cst = np.concatenate(blocks, axis=0).astype(np.float32)
    return np.concatenate([cst, cst], axis=1), level, n_levels


def _dot_nt(a, b):
    return lax.dot_general(a, b, (((1,), (1,)), ((), ())), preferred_element_type=F32)


def _dot_tn(a, b):
    return lax.dot_general(a, b, (((0,), (0,)), ((), ())), preferred_element_type=F32)


def _level_exponents(p, p_ref, lf, ex, c, n_levels):
    out = []
    for l in range(n_levels):
        m = c >> (l + 1)
        if m >= SUBLANES:
            pieces = []
            for b in range(c // (2 * m)):
                lo_row = 2 * m * b
                pr = p_ref[lo_row + m - 1:lo_row + m, :]
                pieces.append(pr - p[lo_row:lo_row + m])
                pieces.append(p[lo_row + m:lo_row + 2 * m] - pr)
            out.append(jnp.concatenate(pieces, axis=0))
        elif m in MXU_LEVEL_HALF_SIZES:
            i = 1 + MXU_LEVEL_HALF_SIZES.index(m)
            out.append(ex[i * c:(i + 1) * c])
        else:
            odd = (lax.broadcasted_iota(jnp.int32, lf.shape, 0) & 1) == 1
            out.append(jnp.where(odd, lf, 0.0))
    return out


def _hgrn_kernel(q_ref, k_ref, v_ref, g_ref, lf_ref, cm_ref, lv_ref, gn_ref, o_ref, st_ref, p_ref, *, c, n_levels):
    @pl.when(pl.program_id(1) == 0)
    def _():
        st_ref[...] = jnp.zeros_like(st_ref)

    t_blk, w = q_ref.shape
    n_heads = w // HEAD_DIM
    lv = lv_ref[...]
    gn = gn_ref[...]

    def chunk(ci, carry):
        r0 = pl.multiple_of(ci * c, c)
        rows = pl.ds(r0, c)
        lf = lf_ref[rows, :]
        hi = lf.astype(BF16)
        lo = (lf - hi.astype(F32)).astype(BF16)
        ex = jnp.dot(cm_ref[...], jnp.concatenate([hi, lo], axis=0), preferred_element_type=F32)
        p = ex[0:c]
        p_ref[...] = p
        dec_q = jnp.exp(p).astype(BF16)
        p_last = p_ref[c - 1:c, :]
        dec_k = jnp.exp(p_last - p).astype(BF16)
        dec_end = jnp.exp(p_last)
        dec_l = [jnp.exp(e).astype(BF16) for e in _level_exponents(p, p_ref, lf, ex, c, n_levels)]
        for h in range(n_heads):
            sl = slice(h * HEAD_DIM, (h + 1) * HEAD_DIM)
            q = q_ref[rows, sl]
            k = k_ref[rows, sl]
            v = v_ref[rows, sl]
            st = st_ref[h]
            o = _dot_nt(q * dec_q[:, sl], st.astype(BF16))
            kt = k.T
            s = jnp.where(lv == n_levels, jnp.dot(q, kt, preferred_element_type=F32), 0.0)
            for l in range(n_levels):
                d = dec_l[l][:, sl]
                s = jnp.where(lv == l, jnp.dot(q * d, kt * d.T, preferred_element_type=F32), s)
            o = o + jnp.dot(s.astype(BF16), v, preferred_element_type=F32)
            st_ref[h] = st * dec_end[:, sl] + _dot_tn(v, k * dec_k[:, sl])
            ms = jnp.mean(o * o, axis=-1, keepdims=True)
            o = o * lax.rsqrt(ms + EPS) * gn * g_ref[rows, sl].astype(F32)
            o_ref[rows, sl] = o.astype(o_ref.dtype)
        return carry

    lax.fori_loop(0, t_blk // c, chunk, 0, unroll=True)


def _hgrn_recurrence(q, k, v, g, lf, gnorm, *, c=HGRN_CHUNK, t_blk=1024, w=512):
    s, d = q.shape
    c = min(c, s)
    assert c >= 2 * SUBLANES and c & (c - 1) == 0, "chunk must be a power of two of at least two sublane tiles"
    t_blk = min(t_blk, s)
    w = min(w, d)
    cm, lv, n_levels = _hgrn_constants(c)
    blk = pl.BlockSpec((t_blk, w), lambda hi, ti: (ti, hi))
    return pl.pallas_call(
        functools.partial(_hgrn_kernel, c=c, n_levels=n_levels),
        grid=(d // w, s // t_blk),
        in_specs=[blk, blk, blk, blk, blk,
                  pl.BlockSpec(cm.shape, lambda hi, ti: (0, 0)),
                  pl.BlockSpec(lv.shape, lambda hi, ti: (0, 0)),
                  pl.BlockSpec((1, HEAD_DIM), lambda hi, ti: (0, 0))],
        out_specs=blk,
        out_shape=jax.ShapeDtypeStruct((s, d), BF16),
        scratch_shapes=[pltpu.VMEM((w // HEAD_DIM, HEAD_DIM, HEAD_DIM), F32), pltpu.VMEM((c, w), F32)],
        compiler_params=_params(("parallel", "arbitrary")),
        name="hgrn_recurrence",
    )(q, k, v, g, lf, jnp.asarray(cm, BF16), jnp.asarray(lv), gnorm.reshape(1, HEAD_DIM).astype(F32))


def _gmlp_kernel(z_ref, lng_ref, lnb_ref, wsp_ref, bsp_ref, o_ref, wc_ref):
    @pl.when(pl.program_id(0) == 0)
    def _():
        n = wsp_ref.shape[1]
        tri = lax.broadcasted_iota(jnp.int32, (n, n), 0) >= lax.broadcasted_iota(jnp.int32, (n, n), 1)
        for g in range(wsp_ref.shape[0]):
            wc_ref[g] = jnp.where(tri, wsp_ref[g], 0.0).astype(BF16)

    t_blk, d2 = z_ref.shape
    d = d2 // 2
    n_groups = d // HEAD_DIM
    for ci in range(t_blk // GMLP_CHUNK):
        rows = slice(ci * GMLP_CHUNK, (ci + 1) * GMLP_CHUNK)
        v = z_ref[rows, d:].astype(F32)
        mu = jnp.mean(v, axis=-1, keepdims=True)
        vc = v - mu
        var = jnp.mean(vc * vc, axis=-1, keepdims=True)
        rstd = lax.rsqrt(var + EPS)
        for g in range(n_groups):
            sl = slice(g * HEAD_DIM, (g + 1) * HEAD_DIM)
            vn = vc[:, sl] * rstd * lng_ref[:, sl] + lnb_ref[:, sl]
            mixed = jnp.dot(wc_ref[g], vn.astype(BF16), preferred_element_type=F32) + bsp_ref[g]
            o_ref[rows, sl] = (z_ref[rows, sl].astype(F32) * mixed).astype(o_ref.dtype)


def _gmlp_mix(z, ln_g, ln_b, w_sp, b_sp, *, t_blk=256):
    s, d2 = z.shape
    d = d2 // 2
    g, n, _ = w_sp.shape
    t_blk = min(t_blk, s)
    bsp = jnp.broadcast_to(b_sp.astype(F32)[:, :, None], (g, n, HEAD_DIM))
    return pl.pallas_call(
        _gmlp_kernel,
        grid=(s // t_blk,),
        in_specs=[pl.BlockSpec((t_blk, d2), lambda i: (i, 0)),
                  pl.BlockSpec((1, d), lambda i: (0, 0)),
                  pl.BlockSpec((1, d), lambda i: (0, 0)),
                  pl.BlockSpec((g, n, n), lambda i: (0, 0, 0)),
                  pl.BlockSpec((g, n, HEAD_DIM), lambda i: (0, 0, 0))],
        out_specs=pl.BlockSpec((t_blk, d), lambda i: (i, 0)),
        out_shape=jax.ShapeDtypeStruct((s, d), BF16),
        scratch_shapes=[pltpu.VMEM((g, n, n), BF16)],
        compiler_params=_params(("arbitrary",)),
        name="gmlp_mix",
    )(z, ln_g.reshape(1, d).astype(F32), ln_b.reshape(1, d).astype(F32), w_sp.astype(F32), bsp)


def kernel(x, norm_mix, norm_ffn, hgrn_w_in, hgrn_lower_bounds, hgrn_gnorm, hgrn_w_out, gmlp_w_in,
           gmlp_ln_gain, gmlp_ln_bias, gmlp_w_spatial, gmlp_b_spatial, gmlp_w_out, ffn_w1, ffn_w2, norm_final):
    bsz, s, d = x.shape
    assert bsz == 1, "the recurrence kernel carries one sequence"
    depth = norm_mix.shape[0]
    h = x.reshape(s, d).astype(F32)
    lower_bounds = _lower_bounds(hgrn_lower_bounds)
    hb, ss = _row_stats(h)

    w_in_b = _cast_fold(hgrn_w_in, 0, norm_mix[0])
    w1_b = None
    for layer in range(depth):
        j = layer // N_MIXERS
        first_w1 = () if w1_b is not None else (_Side(ffn_w1, layer, norm_ffn[layer]),)
        if layer % N_MIXERS == 0:
            lb = lower_bounds[layer].reshape(1, d)
            q, w_out_b = _proj(hb, ss, w_in_b, n_off=0, n=d, epilogue=_ep_silu, out_dtypes=(BF16,),
                               sides=(_Side(hgrn_w_out, j),), name="hgrn_q")
            lf, k = _proj(hb, ss, w_in_b, n_off=d, n=d, epilogue=_ep_forget, extra_rows=(lb,),
                          out_dtypes=(F32, BF16), name="hgrn_f")
            v, *w1_new = _proj(hb, ss, w_in_b, n_off=2 * d, n=d, epilogue=_ep_identity, out_dtypes=(BF16,),
                               sides=first_w1, bn=512 if first_w1 else 1024, name="hgrn_i")
            (g,) = _proj(hb, ss, w_in_b, n_off=3 * d, n=d, epilogue=_ep_silu, out_dtypes=(BF16,), name="hgrn_g")
            mixed = _hgrn_recurrence(q, k, v, g, lf, hgrn_gnorm[j])
        else:
            z, w_out_b, *w1_new = _proj(hb, ss, w_in_b, epilogue=_ep_gelu, out_dtypes=(BF16,),
                                        sides=(_Side(gmlp_w_out, j),) + first_w1, name="gmlp_in")
            mixed = _gmlp_mix(z, gmlp_ln_gain[j], gmlp_ln_bias[j], gmlp_w_spatial[j], gmlp_b_spatial[j])
        if w1_new:
            w1_b = w1_new[0]
        h, hb, ss = _out_proj(mixed, w_out_b, h, bn=512, bk=d, name="mix_out")
        up_sides = [_Side(ffn_w2, layer)]
        nxt = layer + 1
        if nxt < depth:
            nxt_w_in = hgrn_w_in if nxt % N_MIXERS == 0 else gmlp_w_in
            up_sides += [_Side(nxt_w_in, nxt // N_MIXERS, norm_mix[nxt]), _Side(ffn_w1, nxt, norm_ffn[nxt])]
        hid, w2_b, *nxt_w = _proj(hb, ss, w1_b, epilogue=_ep_relu2, out_dtypes=(BF16,), sides=tuple(up_sides),
                                  name="ffn_up")
        w_in_b, w1_b = nxt_w if nxt_w else (None, None)
        h, hb, ss = _out_proj(hid, w2_b, h, name="ffn_down")
    out = _rmsnorm(h, norm_final, F32)
    return out.reshape(bsz, s, d)
```
